```python
import jax, jax.numpy as jnp
from jax import lax
import numpy as np

D_MODEL = 1024
BATCH = 8
SEQ = 2048
DEPTH = 4

GRID_W = 64
N_Q_HEADS = 8
N_KV_HEADS = 2
GROUP = N_Q_HEADS // N_KV_HEADS
HEAD_DIM = 64
ATTN_WIDTH = N_Q_HEADS * HEAD_DIM
KV_WIDTH = N_KV_HEADS * HEAD_DIM
CONV_WIDTH = D_MODEL // 2
CONV_KERNEL = 31
Q_BLOCK = 128
ROPE_THETA = 10000.0
ROPE_AXIS_DIM = HEAD_DIM // 2
D_FF = -(-8 * D_MODEL // (3 * 256)) * 256
IN_WIDTH = ATTN_WIDTH + 2 * KV_WIDTH + 2 * CONV_WIDTH + 2 * D_MODEL
EPS = 1e-6

kernel_name = "hybrid_conformer_gqa_axialrope_adaln_encoder"


def rms_norm(x, g):
    xf = x.astype(jnp.float32)
    y = xf * lax.rsqrt(jnp.mean(xf * xf, axis=-1, keepdims=True) + EPS)
    return (y * g.astype(jnp.float32)).astype(x.dtype)


def layer_norm(x, g, b):
    xf = x.astype(jnp.float32)
    mu = jnp.mean(xf, axis=-1, keepdims=True)
    var = jnp.mean(jnp.square(xf - mu), axis=-1, keepdims=True)
    y = (xf - mu) * lax.rsqrt(var + EPS)
    return (y * g.astype(jnp.float32) + b.astype(jnp.float32)).astype(x.dtype)


def axial_rope_tables(seq_len):
    rows = seq_len // GRID_W
    row_pos = jnp.broadcast_to(jnp.arange(rows)[:, None], (rows, GRID_W)).reshape(-1).astype(jnp.float32)
    col_pos = jnp.broadcast_to(jnp.arange(GRID_W)[None, :], (rows, GRID_W)).reshape(-1).astype(jnp.float32)
    inv_freq = ROPE_THETA ** (-jnp.arange(0, ROPE_AXIS_DIM, 2, dtype=jnp.float32) / ROPE_AXIS_DIM)
    ang_r = row_pos[:, None] * inv_freq[None, :]
    ang_c = col_pos[:, None] * inv_freq[None, :]
    return jnp.cos(ang_r), jnp.sin(ang_r), jnp.cos(ang_c), jnp.sin(ang_c)


def rotate_segment(x, cos, sin):
    x1, x2 = jnp.split(x, 2, axis=-1)
    cos = cos[None, :, None, :]
    sin = sin[None, :, None, :]
    return jnp.concatenate([x1 * cos - x2 * sin, x1 * sin + x2 * cos], axis=-1)


def apply_axial_rope(x, tabs):
    cos_r, sin_r, cos_c, sin_c = tabs
    xf = x.astype(jnp.float32)
    out = jnp.concatenate([
        rotate_segment(xf[..., :ROPE_AXIS_DIM], cos_r, sin_r),
        rotate_segment(xf[..., ROPE_AXIS_DIM:], cos_c, sin_c)], axis=-1)
    return out.astype(x.dtype)


def blocked_gqa(q, k, v):
    b, s = q.shape[0], q.shape[1]
    n_blk = s // Q_BLOCK
    q = q * jnp.asarray(HEAD_DIM ** -0.5, q.dtype)
    qb = q.reshape(b, n_blk, Q_BLOCK, N_KV_HEADS, GROUP, HEAD_DIM).transpose(1, 0, 2, 3, 4, 5)

    def one_block(q_blk):
        scores = jnp.einsum('bqhgd,bkhd->bhgqk', q_blk, k).astype(jnp.float32)
        probs = jax.nn.softmax(scores, axis=-1).astype(v.dtype)
        return jnp.einsum('bhgqk,bkhd->bqhgd', probs, v)

    o = lax.map(one_block, qb)
    return o.transpose(1, 0, 2, 3, 4, 5).reshape(b, s, ATTN_WIDTH)


def depthwise_conv(u, w, bias):
    pad = CONV_KERNEL // 2
    y = lax.conv_general_dilated(u, w[:, None, :], window_strides=(1,), padding=[(pad, pad)],
                                 dimension_numbers=('NWC', 'WIO', 'NWC'),
                                 feature_group_count=CONV_WIDTH)
    return y + bias


def setup_inputs(seed: int = 0) -> dict:
    key = jax.random.key(seed)
    ks = jax.random.split(key, 24)
    f32 = jnp.float32
    nrm = lambda k, shape, scale: (jax.random.normal(k, shape, f32) * scale)
    gain = lambda k, shape: 1.0 + 0.02 * jax.random.normal(k, shape, f32)
    L, D = DEPTH, D_MODEL
    return {
        "x": nrm(ks[0], (BATCH, SEQ, D), 1.0),
        "c": nrm(ks[1], (BATCH, D), 1.0),
        "w_ada": nrm(ks[2], (L, D, 6 * D), 0.5 * D ** -0.5),
        "b_ada": nrm(ks[3], (L, 6 * D), 0.02),
        "norm_mix_g": gain(ks[4], (L, D)),
        "w_in": nrm(ks[5], (L, D, IN_WIDTH), D ** -0.5),
        "q_norm_g": gain(ks[6], (L, HEAD_DIM)),
        "k_norm_g": gain(ks[7], (L, HEAD_DIM)),
        "w_attn_o": nrm(ks[8], (L, ATTN_WIDTH, D), ATTN_WIDTH ** -0.5),
        "conv_dw": nrm(ks[9], (L, CONV_KERNEL, CONV_WIDTH), CONV_KERNEL ** -0.5),
        "conv_dw_b": nrm(ks[10], (L, CONV_WIDTH), 0.02),
        "conv_ln_g": gain(ks[11], (L, CONV_WIDTH)),
        "conv_ln_b": nrm(ks[12], (L, CONV_WIDTH), 0.02),
        "w_conv_o": nrm(ks[13], (L, CONV_WIDTH, D), CONV_WIDTH ** -0.5),
        "b_conv_o": nrm(ks[14], (L, D), 0.02),
        "w_out": nrm(ks[15], (L, D, D), D ** -0.5),
        "norm_ffn_g": gain(ks[16], (L, D)),
        "w_ffn_in": nrm(ks[17], (L, D, 2 * D_FF), D ** -0.5),
        "w_ffn_out": nrm(ks[18], (L, D_FF, D), D_FF ** -0.5),
        "final_norm_g": gain(ks[19], (D,)),
    }


def reference(x, c, w_ada, b_ada, norm_mix_g, w_in, q_norm_g, k_norm_g, w_attn_o,
              conv_dw, conv_dw_b, conv_ln_g, conv_ln_b, w_conv_o, b_conv_o, w_out,
              norm_ffn_g, w_ffn_in, w_ffn_out, final_norm_g):
    b, s, _ = x.shape
    rope_tabs = axial_rope_tables(s)
    c_act = jax.nn.silu(c)
    split_at = (ATTN_WIDTH,
                ATTN_WIDTH + KV_WIDTH,
                ATTN_WIDTH + 2 * KV_WIDTH,
                ATTN_WIDTH + 2 * KV_WIDTH + CONV_WIDTH,
                ATTN_WIDTH + 2 * KV_WIDTH + 2 * CONV_WIDTH,
                ATTN_WIDTH + 2 * KV_WIDTH + 2 * CONV_WIDTH + D_MODEL)

    for l in range(DEPTH):
        mod = (c_act @ w_ada[l] + b_ada[l])[:, None, :]
        shift_m, scale_m, gate_m, shift_f, scale_f, gate_f = jnp.split(mod, 6, axis=-1)

        h = rms_norm(x, norm_mix_g[l]) * (1 + scale_m) + shift_m
        z = h @ w_in[l]
        zq, zk, zv, glu_a, glu_b, zg_conv, zg_attn = jnp.split(z, split_at, axis=-1)

        q = rms_norm(zq.reshape(b, s, N_Q_HEADS, HEAD_DIM), q_norm_g[l])
        k = rms_norm(zk.reshape(b, s, N_KV_HEADS, HEAD_DIM), k_norm_g[l])
        v = zv.reshape(b, s, N_KV_HEADS, HEAD_DIM)
        q = apply_axial_rope(q, rope_tabs)
        k = apply_axial_rope(k, rope_tabs)
        attn_out = blocked_gqa(q, k, v) @ w_attn_o[l]

        u = glu_a * jax.nn.sigmoid(glu_b)
        u = depthwise_conv(u, conv_dw[l], conv_dw_b[l])
        u = jax.nn.silu(layer_norm(u, conv_ln_g[l], conv_ln_b[l]))
        conv_out = u @ w_conv_o[l] + b_conv_o[l]

        merged = jax.nn.sigmoid(zg_conv) * conv_out + jax.nn.sigmoid(zg_attn) * attn_out
        x = x + gate_m * (merged @ w_out[l])

        h = rms_norm(x, norm_ffn_g[l]) * (1 + scale_f) + shift_f
        g_ff, u_ff = jnp.split(h @ w_ffn_in[l], 2, axis=-1)
        x = x + gate_f * ((jax.nn.silu(g_ff) * u_ff) @ w_ffn_out[l])

    return rms_norm(x, final_norm_g)
```

```python
import functools

import jax
import jax.numpy as jnp
from jax import lax
from jax.experimental import pallas as pl
from jax.experimental.pallas import tpu as pltpu

D_MODEL = 1024
GRID_W = 64
N_Q_HEADS = 8
N_KV_HEADS = 2
GROUP = N_Q_HEADS // N_KV_HEADS
HEAD_DIM = 64
ATTN_WIDTH = N_Q_HEADS * HEAD_DIM
KV_WIDTH = N_KV_HEADS * HEAD_DIM
CONV_WIDTH = D_MODEL // 2
CONV_KERNEL = 31
CONV_PAD = CONV_KERNEL // 2
ROPE_THETA = 10000.0
ROPE_AXIS_DIM = HEAD_DIM // 2
D_FF = 2816
IN_WIDTH = ATTN_WIDTH + 2 * KV_WIDTH + 2 * CONV_WIDTH + 2 * D_MODEL
EPS = 1e-6

_OFF_K = ATTN_WIDTH
_OFF_V = _OFF_K + KV_WIDTH
_OFF_GLU_A = _OFF_V + KV_WIDTH
_OFF_GLU_B = _OFF_GLU_A + CONV_WIDTH
_OFF_GATE_CONV = _OFF_GLU_B + CONV_WIDTH
_OFF_GATE_ATTN = _OFF_GATE_CONV + D_MODEL

LANES = 128
SUBLANES = 8
HALO = 16
GROUP_WIDTH = GROUP * HEAD_DIM

TM_MIX = 512
TQ_ATTN = 256
TM_MERGE = 256
TM_FFN = 512
TN_ADA = 1536
CONV_ROWS = 64
FFN_CHUNK = 512
VMEM_LIMIT = 56 * 1024 * 1024

_F32 = jnp.float32
_BF16 = jnp.bfloat16


def _resident(shape):
    nd = len(shape)
    return pl.BlockSpec(shape, lambda *_: (0,) * nd, pipeline_mode=pl.Buffered(1))


def _params(n_axes):
    return pltpu.CompilerParams(dimension_semantics=("arbitrary",) * n_axes,
                                vmem_limit_bytes=VMEM_LIMIT)


def _rms(x):
    return x * lax.rsqrt(jnp.mean(x * x, axis=-1, keepdims=True) + EPS)


def _ada_kernel(c_ref, w_ref, b_ref, o_ref):
    ca = jax.nn.silu(c_ref[...])
    o_ref[0] = jnp.dot(ca, w_ref[0], preferred_element_type=_F32,
                       precision=lax.Precision.HIGHEST) + b_ref[0]


def _ada_call(c, w_ada, b_ada):
    depth, _, width = w_ada.shape
    batch = c.shape[0]
    return pl.pallas_call(
        _ada_kernel,
        grid=(depth, width // TN_ADA),
        in_specs=[
            pl.BlockSpec((batch, D_MODEL), lambda l, j: (0, 0)),
            pl.BlockSpec((1, D_MODEL, TN_ADA), lambda l, j: (l, 0, j)),
            pl.BlockSpec((1, 1, TN_ADA), lambda l, j: (l, 0, j)),
        ],
        out_specs=pl.BlockSpec((1, batch, TN_ADA), lambda l, j: (l, 0, j)),
        out_shape=jax.ShapeDtypeStruct((depth, batch, width), _F32),
        compiler_params=_params(2),
        name="ada_mod",
    )(c, w_ada, b_ada.reshape(depth, 1, width))


def _head_norm_rope(z, bd, g, cos, sa, sb, out_ref, out_scale):
    sq = z * z
    hi = sq.astype(_BF16)
    lo = (sq - hi.astype(_F32)).astype(_BF16)
    ms = (jnp.dot(hi, bd, preferred_element_type=_F32)
          + jnp.dot(lo, bd, preferred_element_type=_F32)) * (1.0 / HEAD_DIM)
    zn = z * lax.rsqrt(ms + EPS)
    for j in range(z.shape[1] // LANES):
        c = zn[:, j * LANES:(j + 1) * LANES] * g
        r = (c * cos + pltpu.roll(c, LANES - ROPE_AXIS_DIM // 2, 1) * sa
             + pltpu.roll(c, ROPE_AXIS_DIM // 2, 1) * sb)
        out_ref[:, j * LANES:(j + 1) * LANES] = (r * out_scale).astype(out_ref.dtype)


def _mix_in_kernel(x_ref, mod_ref, g_ref, w_ref, bd_ref, gq_ref, gk_ref, cos_ref, sa_ref, sb_ref,
                   q_ref, k_ref, v_ref, u_ref, gc_ref, ga_ref):
    mod = mod_ref[0]
    shift, scale = mod[0:1], mod[1:2]
    h = (_rms(x_ref[...]) * g_ref[...]) * (1.0 + scale) + shift
    hb = h.astype(_BF16)

    def proj(off, width):
        return jnp.dot(hb, w_ref[:, off:off + width], preferred_element_type=_F32)

    cos, sa, sb = cos_ref[...], sa_ref[...], sb_ref[...]
    _head_norm_rope(proj(0, ATTN_WIDTH), bd_ref[...], gq_ref[...], cos, sa, sb, q_ref,
                    HEAD_DIM ** -0.5)
    _head_norm_rope(proj(_OFF_K, KV_WIDTH), bd_ref[0:KV_WIDTH, 0:KV_WIDTH], gk_ref[...],
                    cos, sa, sb, k_ref, 1.0)
    v_ref[...] = proj(_OFF_V, KV_WIDTH).astype(v_ref.dtype)
    u_ref[...] = proj(_OFF_GLU_A, CONV_WIDTH) * jax.nn.sigmoid(proj(_OFF_GLU_B, CONV_WIDTH))
    gc_ref[...] = jax.nn.sigmoid(proj(_OFF_GATE_CONV, D_MODEL)).astype(gc_ref.dtype)
    ga_ref[...] = jax.nn.sigmoid(proj(_OFF_GATE_ATTN, D_MODEL)).astype(ga_ref.dtype)


def _mix_in_call(x, mod, g, w_in, bd, gq, gk, cos, sa, sb, seq):
    tokens = x.shape[0]
    tiles_per_seq = seq // TM_MIX
    row = lambda i: (i, 0)
    pos = lambda i: (i % tiles_per_seq, 0)
    return pl.pallas_call(
        _mix_in_kernel,
        grid=(tokens // TM_MIX,),
        in_specs=[
            pl.BlockSpec((TM_MIX, D_MODEL), row),
            pl.BlockSpec((1, 6, D_MODEL), lambda i: (i // tiles_per_seq, 0, 0)),
            _resident((1, D_MODEL)),
            _resident((D_MODEL, IN_WIDTH)),
            _resident((ATTN_WIDTH, ATTN_WIDTH)),
            _resident((1, LANES)),
            _resident((1, LANES)),
            pl.BlockSpec((TM_MIX, LANES), pos),
            pl.BlockSpec((TM_MIX, LANES), pos),
            pl.BlockSpec((TM_MIX, LANES), pos),
        ],
        out_specs=[
            pl.BlockSpec((TM_MIX, ATTN_WIDTH), row),
            pl.BlockSpec((TM_MIX, KV_WIDTH), row),
            pl.BlockSpec((TM_MIX, KV_WIDTH), row),
            pl.BlockSpec((TM_MIX, CONV_WIDTH), row),
            pl.BlockSpec((TM_MIX, D_MODEL), row),
            pl.BlockSpec((TM_MIX, D_MODEL), row),
        ],
        out_shape=[
            jax.ShapeDtypeStruct((tokens, ATTN_WIDTH), _BF16),
            jax.ShapeDtypeStruct((tokens, KV_WIDTH), _BF16),
            jax.ShapeDtypeStruct((tokens, KV_WIDTH), _BF16),
            jax.ShapeDtypeStruct((tokens, CONV_WIDTH), _F32),
            jax.ShapeDtypeStruct((tokens, D_MODEL), _BF16),
            jax.ShapeDtypeStruct((tokens, D_MODEL), _BF16),
        ],
        compiler_params=_params(1),
        name="mix_in",
    )(x, mod, g, w_in, bd, gq, gk, cos, sa, sb)


def _attn_kernel(q_ref, k_ref, v_ref, o_ref, kexp_ref, vexp_ref):
    hk = pl.program_id(1)
    seq = k_ref.shape[0]

    @pl.when(pl.program_id(2) == 0)
    def _():
        lane_head = lax.broadcasted_iota(jnp.int32, (seq, LANES), 1) // HEAD_DIM
        own = lane_head == hk

        def this_head(ref):
            a = ref[...].astype(_F32)
            return jnp.where(own, a, pltpu.roll(a, HEAD_DIM, 1))

        k2 = this_head(k_ref).astype(_BF16)
        kexp_ref[:, 0:LANES] = k2
        kexp_ref[:, LANES:2 * LANES] = k2
        v2 = this_head(v_ref)
        zero = jnp.zeros((seq, LANES), _BF16)
        for g in range(GROUP):
            half, sub = divmod(g, 2)
            vm = jnp.where(lane_head == sub, v2, 0.0).astype(_BF16)
            vexp_ref[g, :, half * LANES:(half + 1) * LANES] = vm
            vexp_ref[g, :, (1 - half) * LANES:(2 - half) * LANES] = zero

    q = q_ref[...]
    lane = lax.broadcasted_iota(jnp.int32, q.shape, 1)
    acc = jnp.zeros(q.shape, _F32)
    inv_l = jnp.zeros(q.shape, _F32)
    for g in range(GROUP):
        in_g = (lane >= g * HEAD_DIM) & (lane < (g + 1) * HEAD_DIM)
        qg = jnp.where(in_g, q, jnp.zeros_like(q))
        s = lax.dot_general(qg, kexp_ref[...], (((1,), (1,)), ((), ())),
                            preferred_element_type=_F32)
        p = jnp.exp(s - jnp.max(s, axis=-1, keepdims=True))
        l = jnp.sum(p, axis=-1, keepdims=True)
        acc = acc + jnp.dot(p.astype(_BF16), vexp_ref[g], preferred_element_type=_F32)
        inv_l = jnp.where(in_g, 1.0 / l, inv_l)
    o_ref[...] = (acc * inv_l).astype(o_ref.dtype)


def _attn_call(q, k, v, batch, seq):
    tokens = q.shape[0]
    nq = seq // TQ_ATTN
    return pl.pallas_call(
        _attn_kernel,
        grid=(batch, N_KV_HEADS, nq),
        in_specs=[
            pl.BlockSpec((TQ_ATTN, GROUP_WIDTH), lambda b, h, i: (b * nq + i, h)),
            pl.BlockSpec((seq, KV_WIDTH), lambda b, h, i: (b, 0)),
            pl.BlockSpec((seq, KV_WIDTH), lambda b, h, i: (b, 0)),
        ],
        out_specs=pl.BlockSpec((TQ_ATTN, GROUP_WIDTH), lambda b, h, i: (b * nq + i, h)),
        out_shape=jax.ShapeDtypeStruct((tokens, ATTN_WIDTH), _BF16),
        scratch_shapes=[
            pltpu.VMEM((seq, GROUP_WIDTH), _BF16),
            pltpu.VMEM((GROUP, seq, GROUP_WIDTH), _BF16),
        ],
        compiler_params=_params(3),
        name="attention",
    )(q, k, v)


def _merge_kernel(x_ref, mod_ref, uc_ref, up_ref, un_ref, o_ref, gc_ref, ga_ref,
                  cw_ref, cb_ref, lg_ref, lb_ref, wco_ref, bco_ref, wao_ref, wout_ref,
                  out_ref, win_ref, y_ref, *, tiles_per_seq):
    tm = uc_ref.shape[0]
    t = pl.program_id(0) % tiles_per_seq
    win_ref[0:HALO] = jnp.where(t == 0, 0.0, up_ref[...])
    win_ref[HALO:HALO + tm] = uc_ref[...]
    win_ref[HALO + tm:] = jnp.where(t == tiles_per_seq - 1, 0.0, un_ref[...])

    for c in range(CONV_WIDTH // LANES):
        lanes = slice(c * LANES, (c + 1) * LANES)
        for r in range(tm // CONV_ROWS):
            acc = jnp.broadcast_to(cb_ref[:, lanes], (CONV_ROWS, LANES))
            for k in range(CONV_KERNEL):
                start = r * CONV_ROWS + HALO - CONV_PAD + k
                acc = acc + win_ref[start:start + CONV_ROWS, lanes] * cw_ref[k:k + 1, lanes]
            y_ref[r * CONV_ROWS:(r + 1) * CONV_ROWS, lanes] = acc

    y = y_ref[...]
    mu = jnp.mean(y, axis=-1, keepdims=True)
    yc = y - mu
    var = jnp.mean(yc * yc, axis=-1, keepdims=True)
    ln = yc * lax.rsqrt(var + EPS) * lg_ref[...] + lb_ref[...]
    conv_out = jnp.dot(jax.nn.silu(ln).astype(_BF16), wco_ref[...],
                       preferred_element_type=_F32) + bco_ref[...]
    attn_out = jnp.dot(o_ref[...], wao_ref[...], preferred_element_type=_F32)
    merged = gc_ref[...].astype(_F32) * conv_out + ga_ref[...].astype(_F32) * attn_out
    gate = mod_ref[0][2:3]
    out_ref[...] = x_ref[...] + gate * jnp.dot(merged.astype(_BF16), wout_ref[...],
                                               preferred_element_type=_F32)


def _merge_call(x, mod, u, o, gc, ga, cw, cb, lg, lb, wco, bco, wao, wout, seq):
    tokens = x.shape[0]
    tm = TM_MERGE
    tiles_per_seq = seq // tm
    halo_per_tile = tm // HALO
    n_halo = tokens // HALO
    row = lambda i: (i, 0)
    return pl.pallas_call(
        functools.partial(_merge_kernel, tiles_per_seq=tiles_per_seq),
        grid=(tokens // tm,),
        in_specs=[
            pl.BlockSpec((tm, D_MODEL), row),
            pl.BlockSpec((1, 6, D_MODEL), lambda i: (i // tiles_per_seq, 0, 0)),
            pl.BlockSpec((tm, CONV_WIDTH), row),
            pl.BlockSpec((HALO, CONV_WIDTH), lambda i: (jnp.maximum(i * halo_per_tile - 1, 0), 0)),
            pl.BlockSpec((HALO, CONV_WIDTH),
                         lambda i: (jnp.minimum((i + 1) * halo_per_tile, n_halo - 1), 0)),
            pl.BlockSpec((tm, ATTN_WIDTH), row),
            pl.BlockSpec((tm, D_MODEL), row),
            pl.BlockSpec((tm, D_MODEL), row),
            _resident((CONV_KERNEL, CONV_WIDTH)),
            _resident((1, CONV_WIDTH)),
            _resident((1, CONV_WIDTH)),
            _resident((1, CONV_WIDTH)),
            _resident((CONV_WIDTH, D_MODEL)),
            _resident((1, D_MODEL)),
            _resident((ATTN_WIDTH, D_MODEL)),
            _resident((D_MODEL, D_MODEL)),
        ],
        out_specs=pl.BlockSpec((tm, D_MODEL), row),
        out_shape=jax.ShapeDtypeStruct((tokens, D_MODEL), _F32),
        scratch_shapes=[
            pltpu.VMEM((tm + 2 * HALO, CONV_WIDTH), _F32),
            pltpu.VMEM((tm, CONV_WIDTH), _F32),
        ],
        compiler_params=_params(1),
        name="merge",
    )(x, mod, u, u, u, o, gc, ga, cw, cb, lg, lb, wco, bco, wao, wout)


def _ffn_kernel(x_ref, mod_ref, g_ref, w1_ref, w2_ref, fg_ref, out_ref, acc_ref, *, final):
    x = x_ref[...]
    mod = mod_ref[0]
    shift, scale, gate = mod[3:4], mod[4:5], mod[5:6]
    hb = ((_rms(x) * g_ref[...]) * (1.0 + scale) + shift).astype(_BF16)
    for c0 in range(0, D_FF, FFN_CHUNK):
        cw = min(FFN_CHUNK, D_FF - c0)
        gg = jnp.dot(hb, w1_ref[:, c0:c0 + cw], preferred_element_type=_F32)
        uu = jnp.dot(hb, w1_ref[:, D_FF + c0:D_FF + c0 + cw], preferred_element_type=_F32)
        act = (jax.nn.silu(gg) * uu).astype(_BF16)
        part = jnp.dot(act, w2_ref[c0:c0 + cw, :], preferred_element_type=_F32)
        if c0 == 0:
            acc_ref[...] = part
        else:
            acc_ref[...] += part
    y = x + gate * acc_ref[...]
    if final:
        y = _rms(y) * fg_ref[...]
    out_ref[...] = y


def _ffn_call(x, mod, g, w1, w2, fg, seq, final):
    tokens = x.shape[0]
    tiles_per_seq = seq // TM_FFN
    row = lambda i: (i, 0)
    return pl.pallas_call(
        functools.partial(_ffn_kernel, final=final),
        grid=(tokens // TM_FFN,),
        in_specs=[
            pl.BlockSpec((TM_FFN, D_MODEL), row),
            pl.BlockSpec((1, 6, D_MODEL), lambda i: (i // tiles_per_seq, 0, 0)),
            _resident((1, D_MODEL)),
            _resident((D_MODEL, 2 * D_FF)),
            _resident((D_FF, D_MODEL)),
            _resident((1, D_MODEL)),
        ],
        out_specs=pl.BlockSpec((TM_FFN, D_MODEL), row),
        out_shape=jax.ShapeDtypeStruct((tokens, D_MODEL), _F32),
        scratch_shapes=[pltpu.VMEM((TM_FFN, D_MODEL), _F32)],
        compiler_params=_params(1),
        name="ffn_final" if final else "ffn",
    )(x, mod, g, w1, w2, fg)


def _rope_tables(seq):
    rows = seq // GRID_W
    row_pos = jnp.broadcast_to(jnp.arange(rows)[:, None], (rows, GRID_W)).reshape(-1).astype(_F32)
    col_pos = jnp.broadcast_to(jnp.arange(GRID_W)[None, :], (rows, GRID_W)).reshape(-1).astype(_F32)
    inv_freq = ROPE_THETA ** (-jnp.arange(0, ROPE_AXIS_DIM, 2, dtype=_F32) / ROPE_AXIS_DIM)
    ang_r = row_pos[:, None] * inv_freq[None, :]
    ang_c = col_pos[:, None] * inv_freq[None, :]
    cos_r, sin_r, cos_c, sin_c = jnp.cos(ang_r), jnp.sin(ang_r), jnp.cos(ang_c), jnp.sin(ang_c)
    zero = jnp.zeros_like(sin_r)
    cos = jnp.concatenate([cos_r, cos_r, cos_c, cos_c], axis=-1)
    sin_a = jnp.concatenate([-sin_r, zero, -sin_c, zero], axis=-1)
    sin_b = jnp.concatenate([zero, sin_r, zero, sin_c], axis=-1)
    reps = LANES // HEAD_DIM
    return jnp.tile(cos, (1, reps)), jnp.tile(sin_a, (1, reps)), jnp.tile(sin_b, (1, reps))


def _head_block_diag():
    head = jnp.arange(ATTN_WIDTH) // HEAD_DIM
    return (head[:, None] == head[None, :]).astype(_BF16)


def kernel(x, c, w_ada, b_ada, norm_mix_g, w_in, q_norm_g, k_norm_g, w_attn_o, conv_dw, conv_dw_b,
           conv_ln_g, conv_ln_b, w_conv_o, b_conv_o, w_out, norm_ffn_g, w_ffn_in, w_ffn_out,
           final_norm_g):
    batch, seq, d = x.shape
    depth = w_in.shape[0]
    assert d == D_MODEL and seq % TM_MIX == 0 and seq % TM_FFN == 0 and seq % TM_MERGE == 0
    assert seq % TQ_ATTN == 0 and seq % GRID_W == 0

    cos, sin_a, sin_b = _rope_tables(seq)
    bd = _head_block_diag()
    mod_all = _ada_call(c, w_ada, b_ada).reshape(depth, batch, 6, D_MODEL)
    w_in_b, w_attn_o_b, w_conv_o_b = w_in.astype(_BF16), w_attn_o.astype(_BF16), w_conv_o.astype(_BF16)
    w_out_b, w_ffn_in_b, w_ffn_out_b = w_out.astype(_BF16), w_ffn_in.astype(_BF16), w_ffn_out.astype(_BF16)
    reps = LANES // HEAD_DIM
    row_vec = lambda a: a.reshape(1, -1)

    xt = x.reshape(batch * seq, d)
    for l in range(depth):
        mod = mod_all[l]
        q, k, v, u, gc, ga = _mix_in_call(
            xt, mod, row_vec(norm_mix_g[l]), w_in_b[l], bd,
            row_vec(jnp.tile(q_norm_g[l], reps)), row_vec(jnp.tile(k_norm_g[l], reps)),
            cos, sin_a, sin_b, seq)
        o = _attn_call(q, k, v, batch, seq)
        xt = _merge_call(xt, mod, u, o, gc, ga, conv_dw[l], row_vec(conv_dw_b[l]),
                         row_vec(conv_ln_g[l]), row_vec(conv_ln_b[l]), w_conv_o_b[l],
                         row_vec(b_conv_o[l]), w_attn_o_b[l], w_out_b[l], seq)
        xt = _ffn_call(xt, mod, row_vec(norm_ffn_g[l]), w_ffn_in_b[l], w_ffn_out_b[l],
                       row_vec(final_norm_g), seq, final=(l == depth - 1))
    return xt.reshape(batch, seq, d)
```

```python
import functools
import math

import jax
import jax.numpy as jnp
from jax import lax
from jax.experimental import pallas as pl
from jax.experimental.pallas import tpu as pltpu

D_MODEL = 1024
GRID_W = 64
N_Q_HEADS = 8
N_KV_HEADS = 2
GROUP = N_Q_HEADS // N_KV_HEADS
HEAD_DIM = 64
ATTN_WIDTH = N_Q_HEADS * HEAD_DIM
KV_WIDTH = N_KV_HEADS * HEAD_DIM
CONV_WIDTH = D_MODEL // 2
CONV_KERNEL = 31
CONV_PAD = CONV_KERNEL // 2
ROPE_THETA = 10000.0
ROPE_AXIS_DIM = HEAD_DIM // 2
D_FF = 2816
IN_WIDTH = ATTN_WIDTH + 2 * KV_WIDTH + 2 * CONV_WIDTH + 2 * D_MODEL
EPS = 1e-6
N_MOD = 6

_OFF_K = ATTN_WIDTH
_OFF_V = _OFF_K + KV_WIDTH
_OFF_GLU_A = _OFF_V + KV_WIDTH
_OFF_GLU_B = _OFF_GLU_A + CONV_WIDTH
_OFF_GATE_CONV = _OFF_GLU_B + CONV_WIDTH
_OFF_GATE_ATTN = _OFF_GATE_CONV + D_MODEL

LANES = 128
SUBLANES = 8
HALO = 16
GROUP_WIDTH = GROUP * HEAD_DIM

TM_MIX = 512
TQ_STEP = 512
TQ_SUB = 256
TM_MERGE = 256
TM_FFN = 512
TN_ADA = 1536
CONV_ROWS = 64
FFN_CHUNK = 512
VMEM_LIMIT = 56 * 1024 * 1024

_Q_SCALE = HEAD_DIM ** -0.5 * math.log2(math.e)

_F32 = jnp.float32
_BF16 = jnp.bfloat16


def _resident(shape):
    nd = len(shape)
    return pl.BlockSpec(shape, lambda *_: (0,) * nd, pipeline_mode=pl.Buffered(1))


def _layer(shape, l):
    nd = len(shape)
    return pl.BlockSpec((1,) + shape, lambda *_: (l,) + (0,) * nd, pipeline_mode=pl.Buffered(1))


def _mod_spec(l, tiles_per_seq):
    return pl.BlockSpec((1, 1, N_MOD, D_MODEL), lambda i: (l, i // tiles_per_seq, 0, 0))


def _params(n_axes):
    return pltpu.CompilerParams(dimension_semantics=("arbitrary",) * n_axes,
                                vmem_limit_bytes=VMEM_LIMIT)


def _rms(x):
    return x * lax.rsqrt(jnp.mean(x * x, axis=-1, keepdims=True) + EPS)


def _ada_kernel(c_ref, w_ref, b_ref, o_ref):
    ca = jax.nn.silu(c_ref[...])
    o_ref[0] = jnp.dot(ca, w_ref[0], preferred_element_type=_F32,
                       precision=lax.Precision.HIGHEST) + b_ref[0]


def _ada_call(c, w_ada, b_ada):
    depth, _, width = w_ada.shape
    batch = c.shape[0]
    return pl.pallas_call(
        _ada_kernel,
        grid=(depth, width // TN_ADA),
        in_specs=[
            pl.BlockSpec((batch, D_MODEL), lambda l, j: (0, 0)),
            pl.BlockSpec((1, D_MODEL, TN_ADA), lambda l, j: (l, 0, j)),
            pl.BlockSpec((1, 1, TN_ADA), lambda l, j: (l, 0, j)),
        ],
        out_specs=pl.BlockSpec((1, batch, TN_ADA), lambda l, j: (l, 0, j)),
        out_shape=jax.ShapeDtypeStruct((depth, batch, width), _F32),
        compiler_params=_params(2),
        name="ada_mod",
    )(c, w_ada, b_ada.reshape(depth, 1, width))


def _head_norm_rope(z, bd, g, cos, sa, sb):
    sq = z * z
    hi = sq.astype(_BF16)
    lo = (sq - hi.astype(_F32)).astype(_BF16)
    ms = (jnp.dot(hi, bd, preferred_element_type=_F32)
          + jnp.dot(lo, bd, preferred_element_type=_F32)) * (1.0 / HEAD_DIM)
    zn = z * lax.rsqrt(ms + EPS)
    out = []
    for j in range(z.shape[1] // LANES):
        c = zn[:, j * LANES:(j + 1) * LANES] * g
        out.append(c * cos + pltpu.roll(c, LANES - ROPE_AXIS_DIM // 2, 1) * sa
                   + pltpu.roll(c, ROPE_AXIS_DIM // 2, 1) * sb)
    return out


def _mix_in_kernel(x_ref, mod_ref, g_ref, w_ref, bd_ref, gq_ref, gk_ref, cos_ref, sa_ref, sb_ref,
                   q_ref, kexp_ref, vaug_ref, u_ref, gc_ref, ga_ref):
    mod = mod_ref[0, 0]
    shift, scale = mod[0:1], mod[1:2]
    h = (_rms(x_ref[...]) * g_ref[0]) * (1.0 + scale) + shift
    hb = h.astype(_BF16)

    def proj(off, width):
        return jnp.dot(hb, w_ref[0, :, off:off + width], preferred_element_type=_F32)

    cos, sa, sb = cos_ref[...], sa_ref[...], sb_ref[...]
    q = _head_norm_rope(proj(0, ATTN_WIDTH), bd_ref[...], gq_ref[0], cos, sa, sb)
    for j, c in enumerate(q):
        q_ref[:, j * LANES:(j + 1) * LANES] = (c * _Q_SCALE).astype(q_ref.dtype)

    k, = _head_norm_rope(proj(_OFF_K, KV_WIDTH), bd_ref[0:KV_WIDTH, 0:KV_WIDTH], gk_ref[0],
                         cos, sa, sb)
    v = proj(_OFF_V, KV_WIDTH)
    low = lax.broadcasted_iota(jnp.int32, k.shape, 1) < HEAD_DIM
    k_sw, v_sw = pltpu.roll(k, HEAD_DIM, 1), pltpu.roll(v, HEAD_DIM, 1)
    for hk, (k2, v1) in enumerate(((jnp.where(low, k, k_sw), v), (jnp.where(low, k_sw, k), v_sw))):
        k2 = k2.astype(kexp_ref.dtype)
        for rep in range(GROUP_WIDTH // LANES):
            off = hk * GROUP_WIDTH + rep * LANES
            kexp_ref[:, off:off + LANES] = k2
        vaug_ref[:, hk * LANES:(hk + 1) * LANES] = jnp.where(low, v1, 1.0).astype(vaug_ref.dtype)

    u_ref[...] = proj(_OFF_GLU_A, CONV_WIDTH) * jax.nn.sigmoid(proj(_OFF_GLU_B, CONV_WIDTH))
    gc_ref[...] = jax.nn.sigmoid(proj(_OFF_GATE_CONV, D_MODEL)).astype(gc_ref.dtype)
    ga_ref[...] = jax.nn.sigmoid(proj(_OFF_GATE_ATTN, D_MODEL)).astype(ga_ref.dtype)


def _mix_in_call(l, x, mod, g, w_in, bd, gq, gk, cos, sa, sb, seq):
    tokens = x.shape[0]
    tiles_per_seq = seq // TM_MIX
    row = lambda i: (i, 0)
    pos = lambda i: (i % tiles_per_seq, 0)
    widths = (ATTN_WIDTH, N_KV_HEADS * GROUP_WIDTH, N_KV_HEADS * LANES, CONV_WIDTH, D_MODEL, D_MODEL)
    dtypes = (_BF16, _BF16, _BF16, _F32, _BF16, _BF16)
    return pl.pallas_call(
        _mix_in_kernel,
        grid=(tokens // TM_MIX,),
        in_specs=[
            pl.BlockSpec((TM_MIX, D_MODEL), row),
            _mod_spec(l, tiles_per_seq),
            _layer((1, D_MODEL), l),
            _layer((D_MODEL, IN_WIDTH), l),
            _resident((ATTN_WIDTH, ATTN_WIDTH)),
            _layer((1, LANES), l),
            _layer((1, LANES), l),
            pl.BlockSpec((TM_MIX, LANES), pos),
            pl.BlockSpec((TM_MIX, LANES), pos),
            pl.BlockSpec((TM_MIX, LANES), pos),
        ],
        out_specs=[pl.BlockSpec((TM_MIX, w), row) for w in widths],
        out_shape=[jax.ShapeDtypeStruct((tokens, w), dt) for w, dt in zip(widths, dtypes)],
        compiler_params=_params(1),
        name="mix_in",
    )(x, mod, g, w_in, bd, gq, gk, cos, sa, sb)


def _attn_kernel(q_ref, kexp_ref, vaug_ref, o_ref):
    kexp = kexp_ref[...]
    vaug = vaug_ref[...]
    lane_head = lax.broadcasted_iota(jnp.int32, (TQ_SUB, GROUP_WIDTH), 1) // HEAD_DIM
    low = lax.broadcasted_iota(jnp.int32, (TQ_SUB, LANES), 1) < HEAD_DIM

    def scores(q):
        stack = jnp.concatenate(
            [jnp.where(lane_head == g, q, jnp.zeros_like(q)) for g in range(GROUP)], axis=0)
        return lax.dot_general(stack, kexp, (((1,), (1,)), ((), ())), preferred_element_type=_F32)

    def finish(s, row0):
        p = jnp.exp2(s - jnp.max(s, axis=-1, keepdims=True)).astype(_BF16)
        acc = jnp.dot(p, vaug, preferred_element_type=_F32)
        r = acc / pltpu.roll(acc, HEAD_DIM, 1)
        head = [r[g * TQ_SUB:(g + 1) * TQ_SUB] for g in range(GROUP)]
        rows = slice(row0, row0 + TQ_SUB)
        for pair in range(GROUP // 2):
            both = jnp.where(low, head[2 * pair], pltpu.roll(head[2 * pair + 1], HEAD_DIM, 1))
            o_ref[rows, pair * LANES:(pair + 1) * LANES] = both.astype(o_ref.dtype)

    subs = range(0, TQ_STEP, TQ_SUB)
    s_all = [scores(q_ref[r0:r0 + TQ_SUB, :]) for r0 in subs]
    for s, r0 in zip(s_all, subs):
        finish(s, r0)


def _attn_call(q, kexp, vaug, batch, seq):
    tokens = q.shape[0]
    nq = seq // TQ_STEP
    return pl.pallas_call(
        _attn_kernel,
        grid=(batch, N_KV_HEADS, nq),
        in_specs=[
            pl.BlockSpec((TQ_STEP, GROUP_WIDTH), lambda b, h, i: (b * nq + i, h)),
            pl.BlockSpec((seq, GROUP_WIDTH), lambda b, h, i: (b, h)),
            pl.BlockSpec((seq, LANES), lambda b, h, i: (b, h)),
        ],
        out_specs=pl.BlockSpec((TQ_STEP, GROUP_WIDTH), lambda b, h, i: (b * nq + i, h)),
        out_shape=jax.ShapeDtypeStruct((tokens, ATTN_WIDTH), _BF16),
        compiler_params=_params(3),
        name="attention",
    )(q, kexp, vaug)


def _merge_kernel(x_ref, mod_ref, uc_ref, up_ref, un_ref, o_ref, gc_ref, ga_ref,
                  cw_ref, cb_ref, lg_ref, lb_ref, wco_ref, bco_ref, wao_ref, wout_ref,
                  out_ref, win_ref, sh_ref, y_ref, *, tiles_per_seq):
    tm = uc_ref.shape[0]
    t = pl.program_id(0) % tiles_per_seq
    win_ref[0:HALO] = jnp.where(t == 0, 0.0, up_ref[...])
    win_ref[HALO:HALO + tm] = uc_ref[...]
    win_ref[HALO + tm:] = jnp.where(t == tiles_per_seq - 1, 0.0, un_ref[...])
    sh_rows = sh_ref.shape[1]
    for r in range(1, SUBLANES):
        sh_ref[r - 1] = win_ref[r:r + sh_rows, :]

    for c in range(CONV_WIDTH // LANES):
        lanes = slice(c * LANES, (c + 1) * LANES)
        for rb in range(0, tm, CONV_ROWS):
            acc = jnp.broadcast_to(cb_ref[0, :, lanes], (CONV_ROWS, LANES))
            for k in range(CONV_KERNEL):
                a, r = divmod(HALO - CONV_PAD + k, SUBLANES)
                rows = slice(rb + a * SUBLANES, rb + a * SUBLANES + CONV_ROWS)
                tap = win_ref[rows, lanes] if r == 0 else sh_ref[r - 1, rows, lanes]
                acc = acc + tap * cw_ref[0, k:k + 1, lanes]
            y_ref[rb:rb + CONV_ROWS, lanes] = acc

    y = y_ref[...]
    mu = jnp.mean(y, axis=-1, keepdims=True)
    yc = y - mu
    var = jnp.mean(yc * yc, axis=-1, keepdims=True)
    ln = yc * lax.rsqrt(var + EPS) * lg_ref[0] + lb_ref[0]
    conv_out = jnp.dot(jax.nn.silu(ln).astype(_BF16), wco_ref[0],
                       preferred_element_type=_F32) + bco_ref[0]
    attn_out = jnp.dot(o_ref[...], wao_ref[0], preferred_element_type=_F32)
    merged = gc_ref[...].astype(_F32) * conv_out + ga_ref[...].astype(_F32) * attn_out
    gate = mod_ref[0, 0][2:3]
    out_ref[...] = x_ref[...] + gate * jnp.dot(merged.astype(_BF16), wout_ref[0],
                                               preferred_element_type=_F32)


def _merge_call(l, x, mod, u, o, gc, ga, cw, cb, lg, lb, wco, bco, wao, wout, seq):
    tokens = x.shape[0]
    tm = TM_MERGE
    tiles_per_seq = seq // tm
    halo_per_tile = tm // HALO
    n_halo = tokens // HALO
    row = lambda i: (i, 0)
    sh_rows = tm + ((CONV_KERNEL + HALO - CONV_PAD - 1) // SUBLANES) * SUBLANES
    return pl.pallas_call(
        functools.partial(_merge_kernel, tiles_per_seq=tiles_per_seq),
        grid=(tokens // tm,),
        in_specs=[
            pl.BlockSpec((tm, D_MODEL), row),
            _mod_spec(l, tiles_per_seq),
            pl.BlockSpec((tm, CONV_WIDTH), row),
            pl.BlockSpec((HALO, CONV_WIDTH), lambda i: (jnp.maximum(i * halo_per_tile - 1, 0), 0)),
            pl.BlockSpec((HALO, CONV_WIDTH),
                         lambda i: (jnp.minimum((i + 1) * halo_per_tile, n_halo - 1), 0)),
            pl.BlockSpec((tm, ATTN_WIDTH), row),
            pl.BlockSpec((tm, D_MODEL), row),
            pl.BlockSpec((tm, D_MODEL), row),
            _layer((CONV_KERNEL, CONV_WIDTH), l),
            _layer((1, CONV_WIDTH), l),
            _layer((1, CONV_WIDTH), l),
            _layer((1, CONV_WIDTH), l),
            _layer((CONV_WIDTH, D_MODEL), l),
            _layer((1, D_MODEL), l),
            _layer((ATTN_WIDTH, D_MODEL), l),
            _layer((D_MODEL, D_MODEL), l),
        ],
        out_specs=pl.BlockSpec((tm, D_MODEL), row),
        out_shape=jax.ShapeDtypeStruct((tokens, D_MODEL), _F32),
        scratch_shapes=[
            pltpu.VMEM((tm + 2 * HALO, CONV_WIDTH), _F32),
            pltpu.VMEM((SUBLANES - 1, sh_rows, CONV_WIDTH), _F32),
            pltpu.VMEM((tm, CONV_WIDTH), _F32),
        ],
        compiler_params=_params(1),
        name="merge",
    )(x, mod, u, u, u, o, gc, ga, cw, cb, lg, lb, wco, bco, wao, wout)


def _ffn_kernel(x_ref, mod_ref, g_ref, w1_ref, w2_ref, fg_ref, out_ref, acc_ref, *, final):
    x = x_ref[...]
    mod = mod_ref[0, 0]
    shift, scale, gate = mod[3:4], mod[4:5], mod[5:6]
    hb = ((_rms(x) * g_ref[0]) * (1.0 + scale) + shift).astype(_BF16)
    for c0 in range(0, D_FF, FFN_CHUNK):
        cw = min(FFN_CHUNK, D_FF - c0)
        gg = jnp.dot(hb, w1_ref[0, :, c0:c0 + cw], preferred_element_type=_F32)
        uu = jnp.dot(hb, w1_ref[0, :, D_FF + c0:D_FF + c0 + cw], preferred_element_type=_F32)
        act = (jax.nn.silu(gg) * uu).astype(_BF16)
        part = jnp.dot(act, w2_ref[0, c0:c0 + cw, :], preferred_element_type=_F32)
        if c0 == 0:
            acc_ref[...] = part
        else:
            acc_ref[...] += part
    y = x + gate * acc_ref[...]
    if final:
        y = _rms(y) * fg_ref[...]
    out_ref[...] = y


def _ffn_call(l, x, mod, g, w1, w2, fg, seq, final):
    tokens = x.shape[0]
    tiles_per_seq = seq // TM_FFN
    row = lambda i: (i, 0)
    return pl.pallas_call(
        functools.partial(_ffn_kernel, final=final),
        grid=(tokens // TM_FFN,),
        in_specs=[
            pl.BlockSpec((TM_FFN, D_MODEL), row),
            _mod_spec(l, tiles_per_seq),
            _layer((1, D_MODEL), l),
            _layer((D_MODEL, 2 * D_FF), l),
            _layer((D_FF, D_MODEL), l),
            _resident((1, D_MODEL)),
        ],
        out_specs=pl.BlockSpec((TM_FFN, D_MODEL), row),
        out_shape=jax.ShapeDtypeStruct((tokens, D_MODEL), _F32),
        scratch_shapes=[pltpu.VMEM((TM_FFN, D_MODEL), _F32)],
        compiler_params=_params(1),
        name="ffn_final" if final else "ffn",
    )(x, mod, g, w1, w2, fg)


def _rope_tables(seq):
    rows = seq // GRID_W
    row_pos = jnp.broadcast_to(jnp.arange(rows)[:, None], (rows, GRID_W)).reshape(-1).astype(_F32)
    col_pos = jnp.broadcast_to(jnp.arange(GRID_W)[None, :], (rows, GRID_W)).reshape(-1).astype(_F32)
    inv_freq = ROPE_THETA ** (-jnp.arange(0, ROPE_AXIS_DIM, 2, dtype=_F32) / ROPE_AXIS_DIM)
    ang_r = row_pos[:, None] * inv_freq[None, :]
    ang_c = col_pos[:, None] * inv_freq[None, :]
    cos_r, sin_r, cos_c, sin_c = jnp.cos(ang_r), jnp.sin(ang_r), jnp.cos(ang_c), jnp.sin(ang_c)
    zero = jnp.zeros_like(sin_r)
    cos = jnp.concatenate([cos_r, cos_r, cos_c, cos_c], axis=-1)
    sin_a = jnp.concatenate([-sin_r, zero, -sin_c, zero], axis=-1)
    sin_b = jnp.concatenate([zero, sin_r, zero, sin_c], axis=-1)
    reps = LANES // HEAD_DIM
    return jnp.tile(cos, (1, reps)), jnp.tile(sin_a, (1, reps)), jnp.tile(sin_b, (1, reps))


def _head_block_diag():
    head = jnp.arange(ATTN_WIDTH) // HEAD_DIM
    return (head[:, None] == head[None, :]).astype(_BF16)


def kernel(x, c, w_ada, b_ada, norm_mix_g, w_in, q_norm_g, k_norm_g, w_attn_o, conv_dw, conv_dw_b,
           conv_ln_g, conv_ln_b, w_conv_o, b_conv_o, w_out, norm_ffn_g, w_ffn_in, w_ffn_out,
           final_norm_g):
    batch, seq, d = x.shape
    depth = w_in.shape[0]
    assert d == D_MODEL and seq % TM_MIX == 0 and seq % TM_FFN == 0 and seq % TM_MERGE == 0
    assert seq % TQ_STEP == 0 and TQ_STEP % TQ_SUB == 0 and seq % GRID_W == 0

    cos, sin_a, sin_b = _rope_tables(seq)
    bd = _head_block_diag()
    mod = _ada_call(c, w_ada, b_ada).reshape(depth, batch, N_MOD, D_MODEL)
    w_in_b, w_attn_o_b, w_conv_o_b = w_in.astype(_BF16), w_attn_o.astype(_BF16), w_conv_o.astype(_BF16)
    w_out_b, w_ffn_in_b, w_ffn_out_b = w_out.astype(_BF16), w_ffn_in.astype(_BF16), w_ffn_out.astype(_BF16)
    vec = lambda a: a.reshape(depth, 1, -1)
    reps = LANES // HEAD_DIM
    gq, gk = vec(jnp.tile(q_norm_g, (1, reps))), vec(jnp.tile(k_norm_g, (1, reps)))
    g_mix, g_ffn, g_final = vec(norm_mix_g), vec(norm_ffn_g), final_norm_g.reshape(1, -1)
    cb, lg, lb, bco = vec(conv_dw_b), vec(conv_ln_g), vec(conv_ln_b), vec(b_conv_o)

    xt = x.reshape(batch * seq, d)
    for l in range(depth):
        q, kexp, vaug, u, gc, ga = _mix_in_call(l, xt, mod, g_mix, w_in_b, bd, gq, gk,
                                                cos, sin_a, sin_b, seq)
        o = _attn_call(q, kexp, vaug, batch, seq)
        xt = _merge_call(l, xt, mod, u, o, gc, ga, conv_dw, cb, lg, lb, w_conv_o_b, bco,
                         w_attn_o_b, w_out_b, seq)
        xt = _ffn_call(l, xt, mod, g_ffn, w_ffn_in_b, w_ffn_out_b, g_final, seq,
                       final=(l == depth - 1))
    return xt.reshape(batch, seq, d)
```

```python
import functools
import math

import jax
import jax.numpy as jnp
from jax import lax
from jax.experimental import pallas as pl
from jax.experimental.pallas import tpu as pltpu

D_MODEL = 1024
GRID_W = 64
N_Q_HEADS = 8
N_KV_HEADS = 2
GROUP = N_Q_HEADS // N_KV_HEADS
HEAD_DIM = 64
ATTN_WIDTH = N_Q_HEADS * HEAD_DIM
KV_WIDTH = N_KV_HEADS * HEAD_DIM
CONV_WIDTH = D_MODEL // 2
CONV_KERNEL = 31
CONV_PAD = CONV_KERNEL // 2
ROPE_THETA = 10000.0
ROPE_AXIS_DIM = HEAD_DIM // 2
D_FF = 2816
IN_WIDTH = ATTN_WIDTH + 2 * KV_WIDTH + 2 * CONV_WIDTH + 2 * D_MODEL
EPS = 1e-6
N_MOD = 6

_OFF_K = ATTN_WIDTH
_OFF_V = _OFF_K + KV_WIDTH
_OFF_GLU_A = _OFF_V + KV_WIDTH
_OFF_GLU_B = _OFF_GLU_A + CONV_WIDTH
_OFF_GATE_CONV = _OFF_GLU_B + CONV_WIDTH
_OFF_GATE_ATTN = _OFF_GATE_CONV + D_MODEL

LANES = 128
SUBLANES = 8
HALO = 16
GROUP_WIDTH = GROUP * HEAD_DIM

TM_MIX = 512
TQ_STEP = 512
TQ_SUB = 256
KEY_TILE = 256
PV_CHUNK = 512
ANCHOR_LAG = 1
TM_FFN = 512
TN_ADA = 1536
CONV_ROWS = 64
FFN_CHUNK = 512
VMEM_LIMIT = 56 * 1024 * 1024

_Q_SCALE = HEAD_DIM ** -0.5 * math.log2(math.e)

_F32 = jnp.float32
_BF16 = jnp.bfloat16


def _resident(shape):
    nd = len(shape)
    return pl.BlockSpec(shape, lambda *_: (0,) * nd, pipeline_mode=pl.Buffered(1))


def _layer(shape, l):
    nd = len(shape)
    return pl.BlockSpec((1,) + shape, lambda *_: (l,) + (0,) * nd, pipeline_mode=pl.Buffered(1))


def _mod_spec(l, tiles_per_seq):
    return pl.BlockSpec((1, 1, N_MOD, D_MODEL), lambda i: (l, i // tiles_per_seq, 0, 0))


def _params(n_axes, flags=None):
    return pltpu.CompilerParams(dimension_semantics=("arbitrary",) * n_axes,
                                vmem_limit_bytes=VMEM_LIMIT, flags=flags)


def _rms(x):
    return x * lax.rsqrt(jnp.mean(x * x, axis=-1, keepdims=True) + EPS)


def _ada_kernel(c_ref, w_ref, b_ref, o_ref):
    ca = jax.nn.silu(c_ref[...])
    o_ref[0] = jnp.dot(ca, w_ref[0], preferred_element_type=_F32,
                       precision=lax.Precision.HIGHEST) + b_ref[0]


def _ada_call(c, w_ada, b_ada):
    depth, _, width = w_ada.shape
    batch = c.shape[0]
    return pl.pallas_call(
        _ada_kernel,
        grid=(depth, width // TN_ADA),
        in_specs=[
            pl.BlockSpec((batch, D_MODEL), lambda l, j: (0, 0)),
            pl.BlockSpec((1, D_MODEL, TN_ADA), lambda l, j: (l, 0, j)),
            pl.BlockSpec((1, 1, TN_ADA), lambda l, j: (l, 0, j)),
        ],
        out_specs=pl.BlockSpec((1, batch, TN_ADA), lambda l, j: (l, 0, j)),
        out_shape=jax.ShapeDtypeStruct((depth, batch, width), _F32),
        compiler_params=_params(2),
        name="ada_mod",
    )(c, w_ada, b_ada.reshape(depth, 1, width))


def _head_norm_rope(z, bd, g, cos, sa, sb):
    sq = (z * z).astype(_BF16)
    wd = bd.shape[0]
    ms = jnp.concatenate([jnp.dot(sq[:, c0:c0 + wd], bd, preferred_element_type=_F32)
                          for c0 in range(0, z.shape[1], wd)], axis=1) * (1.0 / HEAD_DIM)
    zn = z * lax.rsqrt(ms + EPS)
    out = []
    for j in range(z.shape[1] // LANES):
        c = zn[:, j * LANES:(j + 1) * LANES] * g
        out.append(c * cos + pltpu.roll(c, LANES - ROPE_AXIS_DIM // 2, 1) * sa
                   + pltpu.roll(c, ROPE_AXIS_DIM // 2, 1) * sb)
    return out


def _mix_in_kernel(x_ref, mod_ref, g_ref, w_ref, bd_ref, gq_ref, gk_ref, cos_ref, sa_ref, sb_ref,
                   q_ref, kexp_ref, vaug_ref, u_ref, gc_ref, ga_ref):
    mod = mod_ref[0, 0]
    shift, scale = mod[0:1], mod[1:2]
    h = (_rms(x_ref[...]) * g_ref[0]) * (1.0 + scale) + shift
    hb = h.astype(_BF16)

    def proj(off, width):
        return jnp.dot(hb, w_ref[0, :, off:off + width], preferred_element_type=_F32)

    gc_ref[...] = jax.nn.sigmoid(proj(_OFF_GATE_CONV, D_MODEL)).astype(gc_ref.dtype)
    ga_ref[...] = jax.nn.sigmoid(proj(_OFF_GATE_ATTN, D_MODEL)).astype(ga_ref.dtype)
    u_ref[...] = proj(_OFF_GLU_A, CONV_WIDTH) * jax.nn.sigmoid(proj(_OFF_GLU_B, CONV_WIDTH))

    cos, sa, sb = cos_ref[...], sa_ref[...], sb_ref[...]
    q = _head_norm_rope(proj(0, ATTN_WIDTH), bd_ref[...], gq_ref[0], cos, sa, sb)
    for j, c in enumerate(q):
        q_ref[:, j * LANES:(j + 1) * LANES] = (c * _Q_SCALE).astype(q_ref.dtype)

    k, = _head_norm_rope(proj(_OFF_K, KV_WIDTH), bd_ref[0:KV_WIDTH, 0:KV_WIDTH], gk_ref[0],
                         cos, sa, sb)
    v = proj(_OFF_V, KV_WIDTH)
    low = lax.broadcasted_iota(jnp.int32, k.shape, 1) < HEAD_DIM
    k_sw, v_sw = pltpu.roll(k, HEAD_DIM, 1), pltpu.roll(v, HEAD_DIM, 1)
    for hk, (k2, v1) in enumerate(((jnp.where(low, k, k_sw), v), (jnp.where(low, k_sw, k), v_sw))):
        k2 = k2.astype(kexp_ref.dtype)
        for rep in range(GROUP_WIDTH // LANES):
            off = hk * GROUP_WIDTH + rep * LANES
            kexp_ref[:, off:off + LANES] = k2
        vaug_ref[:, hk * LANES:(hk + 1) * LANES] = jnp.where(low, v1, 1.0).astype(vaug_ref.dtype)


def _mix_in_call(l, x, mod, g, w_in, bd, gq, gk, cos, sa, sb, seq):
    tokens = x.shape[0]
    tiles_per_seq = seq // TM_MIX
    row = lambda i: (i, 0)
    pos = lambda i: (i % tiles_per_seq, 0)
    widths = (ATTN_WIDTH, N_KV_HEADS * GROUP_WIDTH, N_KV_HEADS * LANES, CONV_WIDTH, D_MODEL, D_MODEL)
    dtypes = (_BF16, _BF16, _BF16, _F32, _BF16, _BF16)
    return pl.pallas_call(
        _mix_in_kernel,
        grid=(tokens // TM_MIX,),
        in_specs=[
            pl.BlockSpec((TM_MIX, D_MODEL), row),
            _mod_spec(l, tiles_per_seq),
            _layer((1, D_MODEL), l),
            _layer((D_MODEL, IN_WIDTH), l),
            _resident((GROUP_WIDTH, GROUP_WIDTH)),
            _layer((1, LANES), l),
            _layer((1, LANES), l),
            pl.BlockSpec((TM_MIX, LANES), pos),
            pl.BlockSpec((TM_MIX, LANES), pos),
            pl.BlockSpec((TM_MIX, LANES), pos),
        ],
        out_specs=[pl.BlockSpec((TM_MIX, w), row) for w in widths],
        out_shape=[jax.ShapeDtypeStruct((tokens, w), dt) for w, dt in zip(widths, dtypes)],
        compiler_params=_params(1),
        name="mix_in",
    )(x, mod, g, w_in, bd, gq, gk, cos, sa, sb)


def _conv_pieces(uc_ref, up_ref, un_ref, cw_ref, cb_ref, lg_ref, lb_ref, act_ref,
                 win_ref, sh_ref, y_ref, first, last):
    tm = uc_ref.shape[0]

    def window():
        win_ref[0:HALO] = jnp.where(first, 0.0, up_ref[...])
        win_ref[HALO:HALO + tm] = uc_ref[...]
        win_ref[HALO + tm:] = jnp.where(last, 0.0, un_ref[...])
        sh_rows = sh_ref.shape[1]
        for r in range(1, SUBLANES):
            sh_ref[r - 1] = win_ref[r:r + sh_rows, :]

    def block(c, rb):
        lanes = slice(c * LANES, (c + 1) * LANES)
        acc = jnp.broadcast_to(cb_ref[0, :, lanes], (CONV_ROWS, LANES))
        for k in range(CONV_KERNEL):
            a, r = divmod(HALO - CONV_PAD + k, SUBLANES)
            rows = slice(rb + a * SUBLANES, rb + a * SUBLANES + CONV_ROWS)
            tap = win_ref[rows, lanes] if r == 0 else sh_ref[r - 1, rows, lanes]
            acc = acc + tap * cw_ref[0, k:k + 1, lanes]
        y_ref[rb:rb + CONV_ROWS, lanes] = acc
        word = lax.bitcast_convert_type(acc, jnp.int32)
        bits = functools.reduce(jnp.bitwise_or, [word[i:i + SUBLANES]
                                                 for i in range(0, CONV_ROWS, SUBLANES)])
        return lax.shift_right_logical(lax.shift_right_logical(bits, 16), 16)

    def norm_act():
        y = y_ref[...]
        mu = jnp.mean(y, axis=-1, keepdims=True)
        yc = y - mu
        var = jnp.mean(yc * yc, axis=-1, keepdims=True)
        ln = yc * lax.rsqrt(var + EPS) * lg_ref[0] + lb_ref[0]
        act_ref[...] = jax.nn.silu(ln).astype(act_ref.dtype)

    blocks = [functools.partial(block, c, rb)
              for c in range(CONV_WIDTH // LANES) for rb in range(0, tm, CONV_ROWS)]
    return [window] + blocks + [norm_act]


def _spread(items, n):
    bounds = [round(i * len(items) / n) for i in range(n + 1)]
    return [items[bounds[i]:bounds[i + 1]] for i in range(n)]


def _attn_kernel(q_ref, kexp_ref, vaug_ref, uc_ref, up_ref, un_ref, cw_ref, cb_ref, lg_ref, lb_ref,
                 o_ref, act_ref, s_ref, p_ref, acc_ref, win_ref, sh_ref, y_ref,
                 *, conv_tiles_per_seq):
    seq = kexp_ref.shape[0]
    rows = GROUP * TQ_SUB
    assert TQ_STEP == 2 * TQ_SUB
    t = pl.program_id(1) * pl.num_programs(2) + pl.program_id(2)
    conv = _conv_pieces(uc_ref, up_ref, un_ref, cw_ref, cb_ref, lg_ref, lb_ref, act_ref,
                        win_ref, sh_ref, y_ref, t == 0, t == conv_tiles_per_seq - 1)
    lane_head = lax.broadcasted_iota(jnp.int32, (TQ_SUB, GROUP_WIDTH), 1) // HEAD_DIM
    low = lax.broadcasted_iota(jnp.int32, (TQ_SUB, LANES), 1) < HEAD_DIM

    def stacked(sub):
        q = q_ref[sub * TQ_SUB:(sub + 1) * TQ_SUB, :]
        return jnp.concatenate(
            [jnp.where(lane_head == g, q, jnp.zeros_like(q)) for g in range(GROUP)], axis=0)

    def score_tile(sub, stack, j, after=None):
        keys = slice(j * KEY_TILE, (j + 1) * KEY_TILE)
        k_tile = kexp_ref[keys, :]
        if after is not None:
            zero = jnp.tile(after.astype(_F32), (2, 1)).astype(_BF16)
            k_tile = k_tile + jnp.tile(zero, (KEY_TILE // (2 * SUBLANES), GROUP_WIDTH // LANES))
        s_ref[sub, :, keys] = lax.dot_general(stack, k_tile, (((1,), (1,)), ((), ())),
                                              preferred_element_type=_F32)

    def softmax_rows(sub, j, n):
        r = slice(j * rows // n, (j + 1) * rows // n)
        s = s_ref[sub, r, :]
        p_ref[sub, r, :] = jnp.exp2(s - jnp.max(s, axis=-1, keepdims=True)).astype(_BF16)

    def pv_chunk(sub, j):
        keys = slice(j * PV_CHUNK, (j + 1) * PV_CHUNK)
        part = jnp.dot(p_ref[sub, :, keys], vaug_ref[keys, :], preferred_element_type=_F32)
        if j == 0:
            acc_ref[sub] = part
        else:
            acc_ref[sub] += part

    def write_out(sub):
        acc = acc_ref[sub]
        r = acc / pltpu.roll(acc, HEAD_DIM, 1)
        head = [r[g * TQ_SUB:(g + 1) * TQ_SUB] for g in range(GROUP)]
        out_rows = slice(sub * TQ_SUB, (sub + 1) * TQ_SUB)
        for pair in range(GROUP // 2):
            both = jnp.where(low, head[2 * pair], pltpu.roll(head[2 * pair + 1], HEAD_DIM, 1))
            o_ref[out_rows, pair * LANES:(pair + 1) * LANES] = both.astype(o_ref.dtype)

    n_score = seq // KEY_TILE
    n_pv = seq // PV_CHUNK

    stacks = [stacked(0), stacked(1)]
    tokens = []
    for j, pieces in enumerate(_spread(conv[:-1], 2 * n_score)):
        after = tokens[j - 1 - ANCHOR_LAG] if j > ANCHOR_LAG else None
        score_tile(j // n_score, stacks[j // n_score], j % n_score, after=after)
        done = [t for t in (piece() for piece in pieces) if t is not None]
        tokens.append(functools.reduce(jnp.bitwise_or, done) if done else None)
    conv[-1]()
    for sub in range(2):
        for i in range(n_score):
            softmax_rows(sub, i, n_score)
        for j in range(n_pv):
            pv_chunk(sub, j)
        write_out(sub)


def _conv_tile_specs(tm, tile, n_halo):
    halo_per_tile = tm // HALO
    return [
        pl.BlockSpec((tm, CONV_WIDTH), lambda *ids: (tile(*ids), 0)),
        pl.BlockSpec((HALO, CONV_WIDTH),
                     lambda *ids: (jnp.maximum(tile(*ids) * halo_per_tile - 1, 0), 0)),
        pl.BlockSpec((HALO, CONV_WIDTH),
                     lambda *ids: (jnp.minimum((tile(*ids) + 1) * halo_per_tile, n_halo - 1), 0)),
    ]


def _attn_call(l, q, kexp, vaug, u, cw, cb, lg, lb, batch, seq):
    tokens = q.shape[0]
    nq = seq // TQ_STEP
    conv_tiles_per_seq = N_KV_HEADS * nq
    tm = seq // conv_tiles_per_seq
    conv_tile = lambda b, h, i: b * conv_tiles_per_seq + h * nq + i
    sh_rows = tm + ((CONV_KERNEL + HALO - CONV_PAD - 1) // SUBLANES) * SUBLANES
    return pl.pallas_call(
        functools.partial(_attn_kernel, conv_tiles_per_seq=conv_tiles_per_seq),
        grid=(batch, N_KV_HEADS, nq),
        in_specs=[
            pl.BlockSpec((TQ_STEP, GROUP_WIDTH), lambda b, h, i: (b * nq + i, h)),
            pl.BlockSpec((seq, GROUP_WIDTH), lambda b, h, i: (b, h)),
            pl.BlockSpec((seq, LANES), lambda b, h, i: (b, h)),
            *_conv_tile_specs(tm, conv_tile, tokens // HALO),
            _layer((CONV_KERNEL, CONV_WIDTH), l),
            _layer((1, CONV_WIDTH), l),
            _layer((1, CONV_WIDTH), l),
            _layer((1, CONV_WIDTH), l),
        ],
        out_specs=[
            pl.BlockSpec((TQ_STEP, GROUP_WIDTH), lambda b, h, i: (b * nq + i, h)),
            pl.BlockSpec((tm, CONV_WIDTH), lambda b, h, i: (conv_tile(b, h, i), 0)),
        ],
        out_shape=[
            jax.ShapeDtypeStruct((tokens, ATTN_WIDTH), _BF16),
            jax.ShapeDtypeStruct((tokens, CONV_WIDTH), _BF16),
        ],
        scratch_shapes=[
            pltpu.VMEM((2, GROUP * TQ_SUB, seq), _F32),
            pltpu.VMEM((2, GROUP * TQ_SUB, seq), _BF16),
            pltpu.VMEM((2, GROUP * TQ_SUB, LANES), _F32),
            pltpu.VMEM((tm + 2 * HALO, CONV_WIDTH), _F32),
            pltpu.VMEM((SUBLANES - 1, sh_rows, CONV_WIDTH), _F32),
            pltpu.VMEM((tm, CONV_WIDTH), _F32),
        ],
        compiler_params=_params(3),
        name="attention",
    )(q, kexp, vaug, u, u, u, cw, cb, lg, lb)


def _merge_ffn_kernel(x_ref, mod_ref, act_ref, o_ref, gc_ref, ga_ref, wco_ref, bco_ref, wao_ref,
                      wout_ref, g_ref, w1_ref, w2_ref, fg_ref, out_ref, acc_ref, *, final):
    mod = mod_ref[0, 0]
    gate_m, shift, scale, gate_f = mod[2:3], mod[3:4], mod[4:5], mod[5:6]

    conv_out = jnp.dot(act_ref[...], wco_ref[0], preferred_element_type=_F32) + bco_ref[0]
    attn_out = jnp.dot(o_ref[...], wao_ref[0], preferred_element_type=_F32)
    merged = gc_ref[...].astype(_F32) * conv_out + ga_ref[...].astype(_F32) * attn_out
    x = x_ref[...] + gate_m * jnp.dot(merged.astype(_BF16), wout_ref[0],
                                      preferred_element_type=_F32)

    hb = ((_rms(x) * g_ref[0]) * (1.0 + scale) + shift).astype(_BF16)
    for c0 in range(0, D_FF, FFN_CHUNK):
        cw = min(FFN_CHUNK, D_FF - c0)
        gg = jnp.dot(hb, w1_ref[0, :, c0:c0 + cw], preferred_element_type=_F32)
        uu = jnp.dot(hb, w1_ref[0, :, D_FF + c0:D_FF + c0 + cw], preferred_element_type=_F32)
        h2 = (jax.nn.silu(gg) * uu).astype(_BF16)
        part = jnp.dot(h2, w2_ref[0, c0:c0 + cw, :], preferred_element_type=_F32)
        if c0 == 0:
            acc_ref[...] = part
        else:
            acc_ref[...] += part
    y = x + gate_f * acc_ref[...]
    if final:
        y = _rms(y) * fg_ref[...]
    out_ref[...] = y


def _merge_ffn_call(l, x, mod, act, o, gc, ga, wco, bco, wao, wout, g, w1, w2, fg, seq, final):
    tokens = x.shape[0]
    tm = TM_FFN
    row = lambda i: (i, 0)
    return pl.pallas_call(
        functools.partial(_merge_ffn_kernel, final=final),
        grid=(tokens // tm,),
        in_specs=[
            pl.BlockSpec((tm, D_MODEL), row),
            _mod_spec(l, seq // tm),
            pl.BlockSpec((tm, CONV_WIDTH), row),
            pl.BlockSpec((tm, ATTN_WIDTH), row),
            pl.BlockSpec((tm, D_MODEL), row),
            pl.BlockSpec((tm, D_MODEL), row),
            _layer((CONV_WIDTH, D_MODEL), l),
            _layer((1, D_MODEL), l),
            _layer((ATTN_WIDTH, D_MODEL), l),
            _layer((D_MODEL, D_MODEL), l),
            _layer((1, D_MODEL), l),
            _layer((D_MODEL, 2 * D_FF), l),
            _layer((D_FF, D_MODEL), l),
            _resident((1, D_MODEL)),
        ],
        out_specs=pl.BlockSpec((tm, D_MODEL), row),
        out_shape=jax.ShapeDtypeStruct((tokens, D_MODEL), _F32),
        scratch_shapes=[pltpu.VMEM((tm, D_MODEL), _F32)],
        compiler_params=_params(1),
        name="merge_ffn_final" if final else "merge_ffn",
    )(x, mod, act, o, gc, ga, wco, bco, wao, wout, g, w1, w2, fg)


def _rope_tables(seq):
    rows = seq // GRID_W
    row_pos = jnp.broadcast_to(jnp.arange(rows)[:, None], (rows, GRID_W)).reshape(-1).astype(_F32)
    col_pos = jnp.broadcast_to(jnp.arange(GRID_W)[None, :], (rows, GRID_W)).reshape(-1).astype(_F32)
    inv_freq = ROPE_THETA ** (-jnp.arange(0, ROPE_AXIS_DIM, 2, dtype=_F32) / ROPE_AXIS_DIM)
    ang_r = row_pos[:, None] * inv_freq[None, :]
    ang_c = col_pos[:, None] * inv_freq[None, :]
    cos_r, sin_r, cos_c, sin_c = jnp.cos(ang_r), jnp.sin(ang_r), jnp.cos(ang_c), jnp.sin(ang_c)
    zero = jnp.zeros_like(sin_r)
    cos = jnp.concatenate([cos_r, cos_r, cos_c, cos_c], axis=-1)
    sin_a = jnp.concatenate([-sin_r, zero, -sin_c, zero], axis=-1)
    sin_b = jnp.concatenate([zero, sin_r, zero, sin_c], axis=-1)
    reps = LANES // HEAD_DIM
    return jnp.tile(cos, (1, reps)), jnp.tile(sin_a, (1, reps)), jnp.tile(sin_b, (1, reps))


def _head_block_diag():
    head = jnp.arange(GROUP_WIDTH) // HEAD_DIM
    return (head[:, None] == head[None, :]).astype(_BF16)


def kernel(x, c, w_ada, b_ada, norm_mix_g, w_in, q_norm_g, k_norm_g, w_attn_o, conv_dw, conv_dw_b,
           conv_ln_g, conv_ln_b, w_conv_o, b_conv_o, w_out, norm_ffn_g, w_ffn_in, w_ffn_out,
           final_norm_g):
    batch, seq, d = x.shape
    depth = w_in.shape[0]
    assert d == D_MODEL and seq % TM_MIX == 0 and seq % TM_FFN == 0
    assert seq % TQ_STEP == 0 and TQ_STEP % TQ_SUB == 0 and seq % GRID_W == 0

    cos, sin_a, sin_b = _rope_tables(seq)
    bd = _head_block_diag()
    mod = _ada_call(c, w_ada, b_ada).reshape(depth, batch, N_MOD, D_MODEL)
    w_in_b, w_attn_o_b, w_conv_o_b = w_in.astype(_BF16), w_attn_o.astype(_BF16), w_conv_o.astype(_BF16)
    w_out_b, w_ffn_in_b, w_ffn_out_b = w_out.astype(_BF16), w_ffn_in.astype(_BF16), w_ffn_out.astype(_BF16)
    vec = lambda a: a.reshape(depth, 1, -1)
    reps = LANES // HEAD_DIM
    gq, gk = vec(jnp.tile(q_norm_g, (1, reps))), vec(jnp.tile(k_norm_g, (1, reps)))
    g_mix, g_ffn, g_final = vec(norm_mix_g), vec(norm_ffn_g), final_norm_g.reshape(1, -1)
    cb, lg, lb, bco = vec(conv_dw_b), vec(conv_ln_g), vec(conv_ln_b), vec(b_conv_o)

    xt = x.reshape(batch * seq, d)
    for l in range(depth):
        q, kexp, vaug, u, gc, ga = _mix_in_call(l, xt, mod, g_mix, w_in_b, bd, gq, gk,
                                                cos, sin_a, sin_b, seq)
        o, act = _attn_call(l, q, kexp, vaug, u, conv_dw, cb, lg, lb, batch, seq)
        xt = _merge_ffn_call(l, xt, mod, act, o, gc, ga, w_conv_o_b, bco, w_attn_o_b, w_out_b,
                             g_ffn, w_ffn_in_b, w_ffn_out_b, g_final, seq, final=(l == depth - 1))
    return xt.reshape(batch, seq, d)
```

```python
import functools
import math

import jax
import jax.numpy as jnp
from jax import lax
from jax.experimental import pallas as pl
from jax.experimental.pallas import tpu as pltpu

D_MODEL = 1024
GRID_W = 64
N_Q_HEADS = 8
N_KV_HEADS = 2
GROUP = N_Q_HEADS // N_KV_HEADS
HEAD_DIM = 64
ATTN_WIDTH = N_Q_HEADS * HEAD_DIM
KV_WIDTH = N_KV_HEADS * HEAD_DIM
CONV_WIDTH = D_MODEL // 2
CONV_KERNEL = 31
CONV_PAD = CONV_KERNEL // 2
ROPE_THETA = 10000.0
ROPE_AXIS_DIM = HEAD_DIM // 2
D_FF = 2816
IN_WIDTH = ATTN_WIDTH + 2 * KV_WIDTH + 2 * CONV_WIDTH + 2 * D_MODEL
EPS = 1e-6
N_MOD = 6

_OFF_K = ATTN_WIDTH
_OFF_V = _OFF_K + KV_WIDTH
_OFF_GLU_A = _OFF_V + KV_WIDTH
_OFF_GLU_B = _OFF_GLU_A + CONV_WIDTH
_OFF_GATE_CONV = _OFF_GLU_B + CONV_WIDTH
_OFF_GATE_ATTN = _OFF_GATE_CONV + D_MODEL

LANES = 128
SUBLANES = 8
HALO = 16
GROUP_WIDTH = GROUP * HEAD_DIM

TM_MIX = 512
TQ_STEP = 512
TQ_SUB = 256
KEY_TILE = 256
PV_CHUNK = 512
ANCHOR_LAG = 1
TM_FFN = 512
TN_ADA = 1536
CONV_ROWS = 64
FFN_CHUNK = 512
VMEM_LIMIT = 56 * 1024 * 1024

_Q_SCALE = HEAD_DIM ** -0.5 * math.log2(math.e)

_F32 = jnp.float32
_BF16 = jnp.bfloat16


def _resident(shape):
    nd = len(shape)
    return pl.BlockSpec(shape, lambda *_: (0,) * nd, pipeline_mode=pl.Buffered(1))


def _layer(shape, l):
    nd = len(shape)
    return pl.BlockSpec((1,) + shape, lambda *_: (l,) + (0,) * nd, pipeline_mode=pl.Buffered(1))


def _mod_spec(l, tiles_per_seq):
    return pl.BlockSpec((1, 1, N_MOD, D_MODEL), lambda i: (l, i // tiles_per_seq, 0, 0))


def _params(n_axes, flags=None):
    return pltpu.CompilerParams(dimension_semantics=("arbitrary",) * n_axes,
                                vmem_limit_bytes=VMEM_LIMIT, flags=flags)


def _mm(a, w):
    return jnp.dot(a, w.astype(_BF16), preferred_element_type=_F32)


def _rms(x):
    return x * lax.rsqrt(jnp.mean(x * x, axis=-1, keepdims=True) + EPS)


def _split_bf16(a):
    hi = a.astype(_BF16)
    return hi, (a - hi.astype(_F32)).astype(_BF16)


def _ada_kernel(c_ref, w_ref, b_ref, o_ref):
    batch = c_ref.shape[0]
    c_hi, c_lo = _split_bf16(jax.nn.silu(c_ref[...]))
    w_hi, w_lo = _split_bf16(w_ref[0])
    both = jnp.dot(jnp.concatenate([c_hi, c_lo], axis=0), w_hi, preferred_element_type=_F32)
    o_ref[0] = (both[:batch] + both[batch:]
                + jnp.dot(c_hi, w_lo, preferred_element_type=_F32) + b_ref[0])


def _ada_call(c, w_ada, b_ada):
    depth, _, width = w_ada.shape
    batch = c.shape[0]
    return pl.pallas_call(
        _ada_kernel,
        grid=(depth, width // TN_ADA),
        in_specs=[
            pl.BlockSpec((batch, D_MODEL), lambda l, j: (0, 0)),
            pl.BlockSpec((1, D_MODEL, TN_ADA), lambda l, j: (l, 0, j)),
            pl.BlockSpec((1, 1, TN_ADA), lambda l, j: (l, 0, j)),
        ],
        out_specs=pl.BlockSpec((1, batch, TN_ADA), lambda l, j: (l, 0, j)),
        out_shape=jax.ShapeDtypeStruct((depth, batch, width), _F32),
        compiler_params=_params(2),
        name="ada_mod",
    )(c, w_ada, b_ada.reshape(depth, 1, width))


def _head_norm_rope(z, bd, g, cos, sa, sb):
    sq = (z * z).astype(_BF16)
    wd = bd.shape[0]
    ms = jnp.concatenate([jnp.dot(sq[:, c0:c0 + wd], bd, preferred_element_type=_F32)
                          for c0 in range(0, z.shape[1], wd)], axis=1) * (1.0 / HEAD_DIM)
    zn = z * lax.rsqrt(ms + EPS)
    out = []
    for j in range(z.shape[1] // LANES):
        c = zn[:, j * LANES:(j + 1) * LANES] * g
        out.append(c * cos + pltpu.roll(c, LANES - ROPE_AXIS_DIM // 2, 1) * sa
                   + pltpu.roll(c, ROPE_AXIS_DIM // 2, 1) * sb)
    return out


def _mix_in_kernel(x_ref, mod_ref, g_ref, w_ref, bd_ref, gq_ref, gk_ref, cos_ref, sa_ref, sb_ref,
                   q_ref, kexp_ref, vaug_ref, u_ref, gc_ref, ga_ref):
    mod = mod_ref[0, 0]
    shift, scale = mod[0:1], mod[1:2]
    h = (_rms(x_ref[...]) * g_ref[0]) * (1.0 + scale) + shift
    hb = h.astype(_BF16)

    def proj(off, width):
        return _mm(hb, w_ref[0, :, off:off + width])

    zq, zk = proj(0, ATTN_WIDTH), proj(_OFF_K, KV_WIDTH)
    gc_ref[...] = jax.nn.sigmoid(proj(_OFF_GATE_CONV, D_MODEL)).astype(gc_ref.dtype)

    cos, sa, sb = cos_ref[...], sa_ref[...], sb_ref[...]
    q = _head_norm_rope(zq, bd_ref[...], gq_ref[0], cos, sa, sb)
    for j, c in enumerate(q):
        q_ref[:, j * LANES:(j + 1) * LANES] = (c * _Q_SCALE).astype(q_ref.dtype)
    k, = _head_norm_rope(zk, bd_ref[0:KV_WIDTH, 0:KV_WIDTH], gk_ref[0], cos, sa, sb)

    ga_ref[...] = jax.nn.sigmoid(proj(_OFF_GATE_ATTN, D_MODEL)).astype(ga_ref.dtype)
    u_ref[...] = proj(_OFF_GLU_A, CONV_WIDTH) * jax.nn.sigmoid(proj(_OFF_GLU_B, CONV_WIDTH))

    v = proj(_OFF_V, KV_WIDTH)
    low = lax.broadcasted_iota(jnp.int32, k.shape, 1) < HEAD_DIM
    k_sw, v_sw = pltpu.roll(k, HEAD_DIM, 1), pltpu.roll(v, HEAD_DIM, 1)
    for hk, (k2, v1) in enumerate(((jnp.where(low, k, k_sw), v), (jnp.where(low, k_sw, k), v_sw))):
        k2 = k2.astype(kexp_ref.dtype)
        for rep in range(GROUP_WIDTH // LANES):
            off = hk * GROUP_WIDTH + rep * LANES
            kexp_ref[:, off:off + LANES] = k2
        vaug_ref[:, hk * LANES:(hk + 1) * LANES] = jnp.where(low, v1, 1.0).astype(vaug_ref.dtype)


def _mix_in_call(l, x, mod, g, w_in, bd, gq, gk, cos, sa, sb, seq):
    tokens = x.shape[0]
    tiles_per_seq = seq // TM_MIX
    row = lambda i: (i, 0)
    pos = lambda i: (i % tiles_per_seq, 0)
    widths = (ATTN_WIDTH, N_KV_HEADS * GROUP_WIDTH, N_KV_HEADS * LANES, CONV_WIDTH, D_MODEL, D_MODEL)
    dtypes = (_BF16, _BF16, _BF16, _F32, _BF16, _BF16)
    return pl.pallas_call(
        _mix_in_kernel,
        grid=(tokens // TM_MIX,),
        in_specs=[
            pl.BlockSpec((TM_MIX, D_MODEL), row),
            _mod_spec(l, tiles_per_seq),
            _layer((1, D_MODEL), l),
            _layer((D_MODEL, IN_WIDTH), l),
            _resident((GROUP_WIDTH, GROUP_WIDTH)),
            _layer((1, LANES), l),
            _layer((1, LANES), l),
            pl.BlockSpec((TM_MIX, LANES), pos),
            pl.BlockSpec((TM_MIX, LANES), pos),
            pl.BlockSpec((TM_MIX, LANES), pos),
        ],
        out_specs=[pl.BlockSpec((TM_MIX, w), row) for w in widths],
        out_shape=[jax.ShapeDtypeStruct((tokens, w), dt) for w, dt in zip(widths, dtypes)],
        compiler_params=_params(1),
        name="mix_in",
    )(x, mod, g, w_in, bd, gq, gk, cos, sa, sb)


def _conv_pieces(uc_ref, up_ref, un_ref, cw_ref, cb_ref, lg_ref, lb_ref, act_ref,
                 win_ref, sh_ref, y_ref, first, last):
    tm = uc_ref.shape[0]

    def window():
        win_ref[0:HALO] = jnp.where(first, 0.0, up_ref[...])
        win_ref[HALO:HALO + tm] = uc_ref[...]
        win_ref[HALO + tm:] = jnp.where(last, 0.0, un_ref[...])
        sh_rows = sh_ref.shape[1]
        for r in range(1, SUBLANES):
            sh_ref[r - 1] = win_ref[r:r + sh_rows, :]

    def block(c, rb):
        lanes = slice(c * LANES, (c + 1) * LANES)
        acc = jnp.broadcast_to(cb_ref[0, :, lanes], (CONV_ROWS, LANES))
        for k in range(CONV_KERNEL):
            a, r = divmod(HALO - CONV_PAD + k, SUBLANES)
            rows = slice(rb + a * SUBLANES, rb + a * SUBLANES + CONV_ROWS)
            tap = win_ref[rows, lanes] if r == 0 else sh_ref[r - 1, rows, lanes]
            acc = acc + tap * cw_ref[0, k:k + 1, lanes]
        y_ref[rb:rb + CONV_ROWS, lanes] = acc
        word = lax.bitcast_convert_type(acc, jnp.int32)
        bits = functools.reduce(jnp.bitwise_or, [word[i:i + SUBLANES]
                                                 for i in range(0, CONV_ROWS, SUBLANES)])
        return lax.shift_right_logical(lax.shift_right_logical(bits, 16), 16)

    def norm_act():
        y = y_ref[...]
        mu = jnp.mean(y, axis=-1, keepdims=True)
        yc = y - mu
        var = jnp.mean(yc * yc, axis=-1, keepdims=True)
        ln = yc * lax.rsqrt(var + EPS) * lg_ref[0] + lb_ref[0]
        act_ref[...] = jax.nn.silu(ln).astype(act_ref.dtype)

    blocks = [functools.partial(block, c, rb)
              for c in range(CONV_WIDTH // LANES) for rb in range(0, tm, CONV_ROWS)]
    return [window] + blocks + [norm_act]


def _spread(items, n):
    bounds = [round(i * len(items) / n) for i in range(n + 1)]
    return [items[bounds[i]:bounds[i + 1]] for i in range(n)]


def _attn_kernel(q_ref, kexp_ref, vaug_ref, uc_ref, up_ref, un_ref, cw_ref, cb_ref, lg_ref, lb_ref,
                 o_ref, act_ref, s_ref, p_ref, acc_ref, win_ref, sh_ref, y_ref,
                 *, conv_tiles_per_seq):
    seq = kexp_ref.shape[0]
    rows = GROUP * TQ_SUB
    assert TQ_STEP == 2 * TQ_SUB
    t = pl.program_id(1) * pl.num_programs(2) + pl.program_id(2)
    conv = _conv_pieces(uc_ref, up_ref, un_ref, cw_ref, cb_ref, lg_ref, lb_ref, act_ref,
                        win_ref, sh_ref, y_ref, t == 0, t == conv_tiles_per_seq - 1)
    lane_head = lax.broadcasted_iota(jnp.int32, (TQ_SUB, GROUP_WIDTH), 1) // HEAD_DIM
    low = lax.broadcasted_iota(jnp.int32, (TQ_SUB, LANES), 1) < HEAD_DIM

    def stacked(sub):
        q = q_ref[sub * TQ_SUB:(sub + 1) * TQ_SUB, :]
        return jnp.concatenate(
            [jnp.where(lane_head == g, q, jnp.zeros_like(q)) for g in range(GROUP)], axis=0)

    def score_tile(sub, stack, j, after=None):
        keys = slice(j * KEY_TILE, (j + 1) * KEY_TILE)
        k_tile = kexp_ref[keys, :]
        if after is not None:
            zero = jnp.tile(after.astype(_F32), (2, 1)).astype(_BF16)
            k_tile = k_tile + jnp.tile(zero, (KEY_TILE // (2 * SUBLANES), GROUP_WIDTH // LANES))
        s_ref[sub, :, keys] = lax.dot_general(stack, k_tile, (((1,), (1,)), ((), ())),
                                              preferred_element_type=_F32)

    def softmax_rows(sub, j, n):
        r = slice(j * rows // n, (j + 1) * rows // n)
        s = s_ref[sub, r, :]
        p_ref[sub, r, :] = jnp.exp2(s - jnp.max(s, axis=-1, keepdims=True)).astype(_BF16)

    def pv_chunk(sub, j):
        keys = slice(j * PV_CHUNK, (j + 1) * PV_CHUNK)
        part = jnp.dot(p_ref[sub, :, keys], vaug_ref[keys, :], preferred_element_type=_F32)
        if j == 0:
            acc_ref[sub] = part
        else:
            acc_ref[sub] += part

    def write_out(sub):
        acc = acc_ref[sub]
        r = acc / pltpu.roll(acc, HEAD_DIM, 1)
        head = [r[g * TQ_SUB:(g + 1) * TQ_SUB] for g in range(GROUP)]
        out_rows = slice(sub * TQ_SUB, (sub + 1) * TQ_SUB)
        for pair in range(GROUP // 2):
            both = jnp.where(low, head[2 * pair], pltpu.roll(head[2 * pair + 1], HEAD_DIM, 1))
            o_ref[out_rows, pair * LANES:(pair + 1) * LANES] = both.astype(o_ref.dtype)

    n_score = seq // KEY_TILE
    n_pv = seq // PV_CHUNK

    stacks = [stacked(0), stacked(1)]
    tokens = []
    for j, pieces in enumerate(_spread(conv[:-1], 2 * n_score)):
        after = tokens[j - 1 - ANCHOR_LAG] if j > ANCHOR_LAG else None
        score_tile(j // n_score, stacks[j // n_score], j % n_score, after=after)
        done = [t for t in (piece() for piece in pieces) if t is not None]
        tokens.append(functools.reduce(jnp.bitwise_or, done) if done else None)
    conv[-1]()
    for sub in range(2):
        for i in range(n_score):
            softmax_rows(sub, i, n_score)
        for j in range(n_pv):
            pv_chunk(sub, j)
        write_out(sub)


def _conv_tile_specs(tm, tile, n_halo):
    halo_per_tile = tm // HALO
    return [
        pl.BlockSpec((tm, CONV_WIDTH), lambda *ids: (tile(*ids), 0)),
        pl.BlockSpec((HALO, CONV_WIDTH),
                     lambda *ids: (jnp.maximum(tile(*ids) * halo_per_tile - 1, 0), 0)),
        pl.BlockSpec((HALO, CONV_WIDTH),
                     lambda *ids: (jnp.minimum((tile(*ids) + 1) * halo_per_tile, n_halo - 1), 0)),
    ]


def _attn_call(l, q, kexp, vaug, u, cw, cb, lg, lb, batch, seq):
    tokens = q.shape[0]
    nq = seq // TQ_STEP
    conv_tiles_per_seq = N_KV_HEADS * nq
    tm = seq // conv_tiles_per_seq
    conv_tile = lambda b, h, i: b * conv_tiles_per_seq + h * nq + i
    sh_rows = tm + ((CONV_KERNEL + HALO - CONV_PAD - 1) // SUBLANES) * SUBLANES
    return pl.pallas_call(
        functools.partial(_attn_kernel, conv_tiles_per_seq=conv_tiles_per_seq),
        grid=(batch, N_KV_HEADS, nq),
        in_specs=[
            pl.BlockSpec((TQ_STEP, GROUP_WIDTH), lambda b, h, i: (b * nq + i, h)),
            pl.BlockSpec((seq, GROUP_WIDTH), lambda b, h, i: (b, h)),
            pl.BlockSpec((seq, LANES), lambda b, h, i: (b, h)),
            *_conv_tile_specs(tm, conv_tile, tokens // HALO),
            _layer((CONV_KERNEL, CONV_WIDTH), l),
            _layer((1, CONV_WIDTH), l),
            _layer((1, CONV_WIDTH), l),
            _layer((1, CONV_WIDTH), l),
        ],
        out_specs=[
            pl.BlockSpec((TQ_STEP, GROUP_WIDTH), lambda b, h, i: (b * nq + i, h)),
            pl.BlockSpec((tm, CONV_WIDTH), lambda b, h, i: (conv_tile(b, h, i), 0)),
        ],
        out_shape=[
            jax.ShapeDtypeStruct((tokens, ATTN_WIDTH), _BF16),
            jax.ShapeDtypeStruct((tokens, CONV_WIDTH), _BF16),
        ],
        scratch_shapes=[
            pltpu.VMEM((2, GROUP * TQ_SUB, seq), _F32),
            pltpu.VMEM((2, GROUP * TQ_SUB, seq), _BF16),
            pltpu.VMEM((2, GROUP * TQ_SUB, LANES), _F32),
            pltpu.VMEM((tm + 2 * HALO, CONV_WIDTH), _F32),
            pltpu.VMEM((SUBLANES - 1, sh_rows, CONV_WIDTH), _F32),
            pltpu.VMEM((tm, CONV_WIDTH), _F32),
        ],
        compiler_params=_params(3),
        name="attention",
    )(q, kexp, vaug, u, u, u, cw, cb, lg, lb)


def _merge_ffn_kernel(x_ref, mod_ref, act_ref, o_ref, gc_ref, ga_ref, wco_ref, bco_ref, wao_ref,
                      wout_ref, g_ref, w1_ref, w2_ref, fg_ref, out_ref, acc_ref, *, final):
    mod = mod_ref[0, 0]
    gate_m, shift, scale, gate_f = mod[2:3], mod[3:4], mod[4:5], mod[5:6]

    conv_out = _mm(act_ref[...], wco_ref[0]) + bco_ref[0]
    attn_out = _mm(o_ref[...], wao_ref[0])
    merged = gc_ref[...].astype(_F32) * conv_out + ga_ref[...].astype(_F32) * attn_out
    x = x_ref[...] + gate_m * _mm(merged.astype(_BF16), wout_ref[0])

    hb = ((_rms(x) * g_ref[0]) * (1.0 + scale) + shift).astype(_BF16)
    for c0 in range(0, D_FF, FFN_CHUNK):
        cw = min(FFN_CHUNK, D_FF - c0)
        gg = _mm(hb, w1_ref[0, :, c0:c0 + cw])
        uu = _mm(hb, w1_ref[0, :, D_FF + c0:D_FF + c0 + cw])
        h2 = (jax.nn.silu(gg) * uu).astype(_BF16)
        part = _mm(h2, w2_ref[0, c0:c0 + cw, :])
        if c0 == 0:
            acc_ref[...] = part
        else:
            acc_ref[...] += part
    y = x + gate_f * acc_ref[...]
    if final:
        y = _rms(y) * fg_ref[...]
    out_ref[...] = y


def _merge_ffn_call(l, x, mod, act, o, gc, ga, wco, bco, wao, wout, g, w1, w2, fg, seq, final):
    tokens = x.shape[0]
    tm = TM_FFN
    row = lambda i: (i, 0)
    return pl.pallas_call(
        functools.partial(_merge_ffn_kernel, final=final),
        grid=(tokens // tm,),
        in_specs=[
            pl.BlockSpec((tm, D_MODEL), row),
            _mod_spec(l, seq // tm),
            pl.BlockSpec((tm, CONV_WIDTH), row),
            pl.BlockSpec((tm, ATTN_WIDTH), row),
            pl.BlockSpec((tm, D_MODEL), row),
            pl.BlockSpec((tm, D_MODEL), row),
            _layer((CONV_WIDTH, D_MODEL), l),
            _layer((1, D_MODEL), l),
            _layer((ATTN_WIDTH, D_MODEL), l),
            _layer((D_MODEL, D_MODEL), l),
            _layer((1, D_MODEL), l),
            _layer((D_MODEL, 2 * D_FF), l),
            _layer((D_FF, D_MODEL), l),
            _resident((1, D_MODEL)),
        ],
        out_specs=pl.BlockSpec((tm, D_MODEL), row),
        out_shape=jax.ShapeDtypeStruct((tokens, D_MODEL), _F32),
        scratch_shapes=[pltpu.VMEM((tm, D_MODEL), _F32)],
        compiler_params=_params(1),
        name="merge_ffn_final" if final else "merge_ffn",
    )(x, mod, act, o, gc, ga, wco, bco, wao, wout, g, w1, w2, fg)


def _rope_tables(seq):
    rows = seq // GRID_W
    row_pos = jnp.broadcast_to(jnp.arange(rows)[:, None], (rows, GRID_W)).reshape(-1).astype(_F32)
    col_pos = jnp.broadcast_to(jnp.arange(GRID_W)[None, :], (rows, GRID_W)).reshape(-1).astype(_F32)
    inv_freq = ROPE_THETA ** (-jnp.arange(0, ROPE_AXIS_DIM, 2, dtype=_F32) / ROPE_AXIS_DIM)
    ang_r = row_pos[:, None] * inv_freq[None, :]
    ang_c = col_pos[:, None] * inv_freq[None, :]
    cos_r, sin_r, cos_c, sin_c = jnp.cos(ang_r), jnp.sin(ang_r), jnp.cos(ang_c), jnp.sin(ang_c)
    zero = jnp.zeros_like(sin_r)
    cos = jnp.concatenate([cos_r, cos_r, cos_c, cos_c], axis=-1)
    sin_a = jnp.concatenate([-sin_r, zero, -sin_c, zero], axis=-1)
    sin_b = jnp.concatenate([zero, sin_r, zero, sin_c], axis=-1)
    reps = LANES // HEAD_DIM
    return jnp.tile(cos, (1, reps)), jnp.tile(sin_a, (1, reps)), jnp.tile(sin_b, (1, reps))


def _head_block_diag():
    head = jnp.arange(GROUP_WIDTH) // HEAD_DIM
    return (head[:, None] == head[None, :]).astype(_BF16)


def kernel(x, c, w_ada, b_ada, norm_mix_g, w_in, q_norm_g, k_norm_g, w_attn_o, conv_dw, conv_dw_b,
           conv_ln_g, conv_ln_b, w_conv_o, b_conv_o, w_out, norm_ffn_g, w_ffn_in, w_ffn_out,
           final_norm_g):
    batch, seq, d = x.shape
    depth = w_in.shape[0]
    assert d == D_MODEL and seq % TM_MIX == 0 and seq % TM_FFN == 0
    assert seq % TQ_STEP == 0 and TQ_STEP % TQ_SUB == 0 and seq % GRID_W == 0

    cos, sin_a, sin_b = _rope_tables(seq)
    bd = _head_block_diag()
    mod = _ada_call(c, w_ada, b_ada).reshape(depth, batch, N_MOD, D_MODEL)
    w_ffn_in_b = w_ffn_in.astype(_BF16)
    vec = lambda a: a.reshape(depth, 1, -1)
    reps = LANES // HEAD_DIM
    gq, gk = vec(jnp.tile(q_norm_g, (1, reps))), vec(jnp.tile(k_norm_g, (1, reps)))
    g_mix, g_ffn, g_final = vec(norm_mix_g), vec(norm_ffn_g), final_norm_g.reshape(1, -1)
    cb, lg, lb, bco = vec(conv_dw_b), vec(conv_ln_g), vec(conv_ln_b), vec(b_conv_o)

    xt = x.reshape(batch * seq, d)
    for l in range(depth):
        q, kexp, vaug, u, gc, ga = _mix_in_call(l, xt, mod, g_mix, w_in, bd, gq, gk,
                                                cos, sin_a, sin_b, seq)
        o, act = _attn_call(l, q, kexp, vaug, u, conv_dw, cb, lg, lb, batch, seq)
        xt = _merge_ffn_call(l, xt, mod, act, o, gc, ga, w_conv_o, bco, w_attn_o, w_out,
                             g_ffn, w_ffn_in_b, w_ffn_out, g_final, seq, final=(l == depth - 1))
    return xt.reshape(batch, seq, d)
```

```python
import functools
import math

import jax
import jax.numpy as jnp
from jax import lax
from jax.experimental import pallas as pl
from jax.experimental.pallas import tpu as pltpu

D_MODEL = 1024
GRID_W = 64
N_Q_HEADS = 8
N_KV_HEADS = 2
GROUP = N_Q_HEADS // N_KV_HEADS
HEAD_DIM = 64
ATTN_WIDTH = N_Q_HEADS * HEAD_DIM
KV_WIDTH = N_KV_HEADS * HEAD_DIM
CONV_WIDTH = D_MODEL // 2
CONV_KERNEL = 31
CONV_PAD = CONV_KERNEL // 2
ROPE_THETA = 10000.0
ROPE_AXIS_DIM = HEAD_DIM // 2
D_FF = 2816
IN_WIDTH = ATTN_WIDTH + 2 * KV_WIDTH + 2 * CONV_WIDTH + 2 * D_MODEL
EPS = 1e-6
N_MOD = 6

_OFF_K = ATTN_WIDTH
_OFF_V = _OFF_K + KV_WIDTH
_OFF_GLU_A = _OFF_V + KV_WIDTH
_OFF_GLU_B = _OFF_GLU_A + CONV_WIDTH
_OFF_GATE_CONV = _OFF_GLU_B + CONV_WIDTH
_OFF_GATE_ATTN = _OFF_GATE_CONV + D_MODEL

LANES = 128
SUBLANES = 8
HALO = 16
GROUP_WIDTH = GROUP * HEAD_DIM

TM_MIX = 512
TQ_STEP = 512
TQ_SUB = 256
KEY_TILE = 256
PV_CHUNK = 512
PV_ROW_SPLIT = 2
ANCHOR_LAG = 1
TM_FFN = 512
TN_ADA = 1536
CONV_ROWS = 64
FFN_CHUNK = 512
VMEM_LIMIT = 56 * 1024 * 1024

_Q_SCALE = HEAD_DIM ** -0.5 * math.log2(math.e)

_F32 = jnp.float32
_BF16 = jnp.bfloat16


def _resident(shape):
    nd = len(shape)
    return pl.BlockSpec(shape, lambda *_: (0,) * nd, pipeline_mode=pl.Buffered(1))


def _layer(shape, l):
    nd = len(shape)
    return pl.BlockSpec((1,) + shape, lambda *_: (l,) + (0,) * nd, pipeline_mode=pl.Buffered(1))


def _mod_spec(l, tiles_per_seq):
    return pl.BlockSpec((1, 1, N_MOD, D_MODEL), lambda i: (l, i // tiles_per_seq, 0, 0))


def _params(n_axes, flags=None):
    return pltpu.CompilerParams(dimension_semantics=("arbitrary",) * n_axes,
                                vmem_limit_bytes=VMEM_LIMIT, flags=flags)


def _mm(a, w):
    return jnp.dot(a, w.astype(_BF16), preferred_element_type=_F32)


def _rms(x):
    return x * lax.rsqrt(jnp.mean(x * x, axis=-1, keepdims=True) + EPS)


def _split_bf16(a):
    hi = a.astype(_BF16)
    return hi, (a - hi.astype(_F32)).astype(_BF16)


def _ada_kernel(c_ref, w_ref, b_ref, o_ref):
    batch = c_ref.shape[0]
    c_hi, c_lo = _split_bf16(jax.nn.silu(c_ref[...]))
    w_hi, w_lo = _split_bf16(w_ref[0])
    both = jnp.dot(jnp.concatenate([c_hi, c_lo], axis=0), w_hi, preferred_element_type=_F32)
    o_ref[0] = (both[:batch] + both[batch:]
                + jnp.dot(c_hi, w_lo, preferred_element_type=_F32) + b_ref[0])


def _ada_call(c, w_ada, b_ada):
    depth, _, width = w_ada.shape
    batch = c.shape[0]
    return pl.pallas_call(
        _ada_kernel,
        grid=(depth, width // TN_ADA),
        in_specs=[
            pl.BlockSpec((batch, D_MODEL), lambda l, j: (0, 0)),
            pl.BlockSpec((1, D_MODEL, TN_ADA), lambda l, j: (l, 0, j)),
            pl.BlockSpec((1, 1, TN_ADA), lambda l, j: (l, 0, j)),
        ],
        out_specs=pl.BlockSpec((1, batch, TN_ADA), lambda l, j: (l, 0, j)),
        out_shape=jax.ShapeDtypeStruct((depth, batch, width), _F32),
        compiler_params=_params(2),
        name="ada_mod",
    )(c, w_ada, b_ada.reshape(depth, 1, width))


def _head_norm_rope(z, bd, g, cos, sa, sb):
    sq = (z * z).astype(_BF16)
    wd = bd.shape[0]
    ms = jnp.concatenate([jnp.dot(sq[:, c0:c0 + wd], bd, preferred_element_type=_F32)
                          for c0 in range(0, z.shape[1], wd)], axis=1) * (1.0 / HEAD_DIM)
    zn = z * lax.rsqrt(ms + EPS)
    out = []
    for j in range(z.shape[1] // LANES):
        c = zn[:, j * LANES:(j + 1) * LANES] * g
        out.append(c * cos + pltpu.roll(c, LANES - ROPE_AXIS_DIM // 2, 1) * sa
                   + pltpu.roll(c, ROPE_AXIS_DIM // 2, 1) * sb)
    return out


def _mix_in_kernel(x_ref, mod_ref, g_ref, w_ref, bd_ref, gq_ref, gk_ref, cos_ref, sa_ref, sb_ref,
                   wf_ref, q_ref, kexp_ref, vaug_ref, u_ref, gc_ref, ga_ref, wfb_ref):
    wfb_ref[0] = wf_ref[0].astype(wfb_ref.dtype)

    mod = mod_ref[0, 0]
    shift, scale = mod[0:1], mod[1:2]
    h = (_rms(x_ref[...]) * g_ref[0]) * (1.0 + scale) + shift
    hb = h.astype(_BF16)

    def proj(off, width):
        return _mm(hb, w_ref[0, :, off:off + width])

    zq, zk = proj(0, ATTN_WIDTH), proj(_OFF_K, KV_WIDTH)
    gc_ref[...] = jax.nn.sigmoid(proj(_OFF_GATE_CONV, D_MODEL)).astype(gc_ref.dtype)

    cos, sa, sb = cos_ref[...], sa_ref[...], sb_ref[...]
    q = _head_norm_rope(zq, bd_ref[...], gq_ref[0], cos, sa, sb)
    for j, c in enumerate(q):
        q_ref[:, j * LANES:(j + 1) * LANES] = (c * _Q_SCALE).astype(q_ref.dtype)
    k, = _head_norm_rope(zk, bd_ref[0:KV_WIDTH, 0:KV_WIDTH], gk_ref[0], cos, sa, sb)

    ga_ref[...] = jax.nn.sigmoid(proj(_OFF_GATE_ATTN, D_MODEL)).astype(ga_ref.dtype)
    u_ref[...] = proj(_OFF_GLU_A, CONV_WIDTH) * jax.nn.sigmoid(proj(_OFF_GLU_B, CONV_WIDTH))

    v = proj(_OFF_V, KV_WIDTH)
    low = lax.broadcasted_iota(jnp.int32, k.shape, 1) < HEAD_DIM
    k_sw, v_sw = pltpu.roll(k, HEAD_DIM, 1), pltpu.roll(v, HEAD_DIM, 1)
    for hk, (k2, v1) in enumerate(((jnp.where(low, k, k_sw), v), (jnp.where(low, k_sw, k), v_sw))):
        k2 = k2.astype(kexp_ref.dtype)
        for rep in range(GROUP_WIDTH // LANES):
            off = hk * GROUP_WIDTH + rep * LANES
            kexp_ref[:, off:off + LANES] = k2
        vaug_ref[:, hk * LANES:(hk + 1) * LANES] = jnp.where(low, v1, 1.0).astype(vaug_ref.dtype)


def _mix_in_call(l, x, mod, g, w_in, bd, gq, gk, cos, sa, sb, w_ffn_in, seq):
    tokens = x.shape[0]
    steps = tokens // TM_MIX
    tiles_per_seq = seq // TM_MIX
    wf_rows = D_MODEL // steps
    row = lambda i: (i, 0)
    pos = lambda i: (i % tiles_per_seq, 0)
    widths = (ATTN_WIDTH, N_KV_HEADS * GROUP_WIDTH, N_KV_HEADS * LANES, CONV_WIDTH, D_MODEL, D_MODEL)
    dtypes = (_BF16, _BF16, _BF16, _F32, _BF16, _BF16)
    return pl.pallas_call(
        _mix_in_kernel,
        grid=(steps,),
        in_specs=[
            pl.BlockSpec((TM_MIX, D_MODEL), row),
            _mod_spec(l, tiles_per_seq),
            _layer((1, D_MODEL), l),
            _layer((D_MODEL, IN_WIDTH), l),
            _resident((GROUP_WIDTH, GROUP_WIDTH)),
            _layer((1, LANES), l),
            _layer((1, LANES), l),
            pl.BlockSpec((TM_MIX, LANES), pos),
            pl.BlockSpec((TM_MIX, LANES), pos),
            pl.BlockSpec((TM_MIX, LANES), pos),
            pl.BlockSpec((1, wf_rows, 2 * D_FF), lambda i: (l, i, 0)),
        ],
        out_specs=[pl.BlockSpec((TM_MIX, w), row) for w in widths]
        + [pl.BlockSpec((1, wf_rows, 2 * D_FF), lambda i: (0, i, 0))],
        out_shape=[jax.ShapeDtypeStruct((tokens, w), dt) for w, dt in zip(widths, dtypes)]
        + [jax.ShapeDtypeStruct((1, D_MODEL, 2 * D_FF), _BF16)],
        compiler_params=_params(1),
        name="mix_in",
    )(x, mod, g, w_in, bd, gq, gk, cos, sa, sb, w_ffn_in)


def _conv_pieces(uc_ref, up_ref, un_ref, cw_ref, cb_ref, y_ref, win_ref, sh_ref, first, last):
    tm = uc_ref.shape[0]

    def window():
        win_ref[0:HALO] = jnp.where(first, 0.0, up_ref[...])
        win_ref[HALO:HALO + tm] = uc_ref[...]
        win_ref[HALO + tm:] = jnp.where(last, 0.0, un_ref[...])
        sh_rows = sh_ref.shape[1]
        for r in range(1, SUBLANES):
            sh_ref[r - 1] = win_ref[r:r + sh_rows, :]

    def block(c, rb):
        lanes = slice(c * LANES, (c + 1) * LANES)
        acc = jnp.broadcast_to(cb_ref[0, :, lanes], (CONV_ROWS, LANES))
        for k in range(CONV_KERNEL):
            a, r = divmod(HALO - CONV_PAD + k, SUBLANES)
            rows = slice(rb + a * SUBLANES, rb + a * SUBLANES + CONV_ROWS)
            tap = win_ref[rows, lanes] if r == 0 else sh_ref[r - 1, rows, lanes]
            acc = acc + tap * cw_ref[0, k:k + 1, lanes]
        y_ref[rb:rb + CONV_ROWS, lanes] = acc
        word = lax.bitcast_convert_type(acc, jnp.int32)
        bits = functools.reduce(jnp.bitwise_or, [word[i:i + SUBLANES]
                                                 for i in range(0, CONV_ROWS, SUBLANES)])
        return lax.shift_right_logical(lax.shift_right_logical(bits, 16), 16)

    blocks = [functools.partial(block, c, rb)
              for c in range(CONV_WIDTH // LANES) for rb in range(0, tm, CONV_ROWS)]
    return [window] + blocks


def _spread(items, n):
    bounds = [round(i * len(items) / n) for i in range(n + 1)]
    return [items[bounds[i]:bounds[i + 1]] for i in range(n)]


def _attn_kernel(q_ref, kexp_ref, vaug_ref, uc_ref, up_ref, un_ref, cw_ref, cb_ref,
                 o_ref, y_ref, s_ref, p_ref, acc_ref, win_ref, sh_ref, *, conv_tiles_per_seq):
    seq = kexp_ref.shape[0]
    rows = GROUP * TQ_SUB
    assert TQ_STEP == 2 * TQ_SUB
    t = pl.program_id(1) * pl.num_programs(2) + pl.program_id(2)
    conv = _conv_pieces(uc_ref, up_ref, un_ref, cw_ref, cb_ref, y_ref, win_ref, sh_ref,
                        t == 0, t == conv_tiles_per_seq - 1)
    lane_head = lax.broadcasted_iota(jnp.int32, (TQ_SUB, GROUP_WIDTH), 1) // HEAD_DIM
    low = lax.broadcasted_iota(jnp.int32, (TQ_SUB, LANES), 1) < HEAD_DIM

    def stacked(sub):
        q = q_ref[sub * TQ_SUB:(sub + 1) * TQ_SUB, :]
        return jnp.concatenate(
            [jnp.where(lane_head == g, q, jnp.zeros_like(q)) for g in range(GROUP)], axis=0)

    def score_tile(sub, stack, j, after=None):
        keys = slice(j * KEY_TILE, (j + 1) * KEY_TILE)
        k_tile = kexp_ref[keys, :]
        if after is not None:
            zero = jnp.tile(after.astype(_F32), (2, 1)).astype(_BF16)
            k_tile = k_tile + jnp.tile(zero, (KEY_TILE // (2 * SUBLANES), GROUP_WIDTH // LANES))
        s_ref[sub, :, keys] = lax.dot_general(stack, k_tile, (((1,), (1,)), ((), ())),
                                              preferred_element_type=_F32)

    def softmax_rows(sub, j, n):
        r = slice(j * rows // n, (j + 1) * rows // n)
        s = s_ref[sub, r, :]
        p_ref[sub, r, :] = jnp.exp2(s - jnp.max(s, axis=-1, keepdims=True)).astype(_BF16)

    def pv_chunk(sub, half, j):
        r = slice(half * rows // PV_ROW_SPLIT, (half + 1) * rows // PV_ROW_SPLIT)
        keys = slice(j * PV_CHUNK, (j + 1) * PV_CHUNK)
        part = jnp.dot(p_ref[sub, r, keys], vaug_ref[keys, :], preferred_element_type=_F32)
        if j == 0:
            acc_ref[sub, r, :] = part
        else:
            acc_ref[sub, r, :] += part

    def write_out(sub):
        acc = acc_ref[sub]
        r = acc / pltpu.roll(acc, HEAD_DIM, 1)
        head = [r[g * TQ_SUB:(g + 1) * TQ_SUB] for g in range(GROUP)]
        out_rows = slice(sub * TQ_SUB, (sub + 1) * TQ_SUB)
        for pair in range(GROUP // 2):
            both = jnp.where(low, head[2 * pair], pltpu.roll(head[2 * pair + 1], HEAD_DIM, 1))
            o_ref[out_rows, pair * LANES:(pair + 1) * LANES] = both.astype(o_ref.dtype)

    n_score = seq // KEY_TILE
    n_pv = seq // PV_CHUNK

    stacks = [stacked(0), stacked(1)]
    tokens = []
    for j, pieces in enumerate(_spread(conv, 2 * n_score)):
        after = tokens[j - 1 - ANCHOR_LAG] if j > ANCHOR_LAG else None
        score_tile(j // n_score, stacks[j // n_score], j % n_score, after=after)
        done = [t for t in (piece() for piece in pieces) if t is not None]
        tokens.append(functools.reduce(jnp.bitwise_or, done) if done else None)
    for sub in range(2):
        for half, chunks in enumerate(_spread(list(range(n_score)), PV_ROW_SPLIT)):
            for i in chunks:
                softmax_rows(sub, i, n_score)
            for j in range(n_pv):
                pv_chunk(sub, half, j)
        write_out(sub)


def _conv_tile_specs(tm, tile, n_halo):
    halo_per_tile = tm // HALO
    return [
        pl.BlockSpec((tm, CONV_WIDTH), lambda *ids: (tile(*ids), 0)),
        pl.BlockSpec((HALO, CONV_WIDTH),
                     lambda *ids: (jnp.maximum(tile(*ids) * halo_per_tile - 1, 0), 0)),
        pl.BlockSpec((HALO, CONV_WIDTH),
                     lambda *ids: (jnp.minimum((tile(*ids) + 1) * halo_per_tile, n_halo - 1), 0)),
    ]


def _attn_call(l, q, kexp, vaug, u, cw, cb, batch, seq):
    tokens = q.shape[0]
    nq = seq // TQ_STEP
    conv_tiles_per_seq = N_KV_HEADS * nq
    tm = seq // conv_tiles_per_seq
    conv_tile = lambda b, h, i: b * conv_tiles_per_seq + h * nq + i
    sh_rows = tm + ((CONV_KERNEL + HALO - CONV_PAD - 1) // SUBLANES) * SUBLANES
    return pl.pallas_call(
        functools.partial(_attn_kernel, conv_tiles_per_seq=conv_tiles_per_seq),
        grid=(batch, N_KV_HEADS, nq),
        in_specs=[
            pl.BlockSpec((TQ_STEP, GROUP_WIDTH), lambda b, h, i: (b * nq + i, h)),
            pl.BlockSpec((seq, GROUP_WIDTH), lambda b, h, i: (b, h)),
            pl.BlockSpec((seq, LANES), lambda b, h, i: (b, h)),
            *_conv_tile_specs(tm, conv_tile, tokens // HALO),
            _layer((CONV_KERNEL, CONV_WIDTH), l),
            _layer((1, CONV_WIDTH), l),
        ],
        out_specs=[
            pl.BlockSpec((TQ_STEP, GROUP_WIDTH), lambda b, h, i: (b * nq + i, h)),
            pl.BlockSpec((tm, CONV_WIDTH), lambda b, h, i: (conv_tile(b, h, i), 0)),
        ],
        out_shape=[
            jax.ShapeDtypeStruct((tokens, ATTN_WIDTH), _BF16),
            jax.ShapeDtypeStruct((tokens, CONV_WIDTH), _F32),
        ],
        scratch_shapes=[
            pltpu.VMEM((2, GROUP * TQ_SUB, seq), _F32),
            pltpu.VMEM((2, GROUP * TQ_SUB, seq), _BF16),
            pltpu.VMEM((2, GROUP * TQ_SUB, LANES), _F32),
            pltpu.VMEM((tm + 2 * HALO, CONV_WIDTH), _F32),
            pltpu.VMEM((SUBLANES - 1, sh_rows, CONV_WIDTH), _F32),
        ],
        compiler_params=_params(3),
        name="attention",
    )(q, kexp, vaug, u, u, u, cw, cb)


def _merge_ffn_kernel(x_ref, mod_ref, y_ref, o_ref, gc_ref, ga_ref, lg_ref, lb_ref, wco_ref, bco_ref,
                      wao_ref, wout_ref, g_ref, w1_ref, w2_ref, fg_ref, out_ref, acc_ref, *, final):
    mod = mod_ref[0, 0]
    gate_m, shift, scale, gate_f = mod[2:3], mod[3:4], mod[4:5], mod[5:6]

    attn_out = _mm(o_ref[...], wao_ref[0])
    y = y_ref[...]
    yc = y - jnp.mean(y, axis=-1, keepdims=True)
    ln = yc * lax.rsqrt(jnp.mean(yc * yc, axis=-1, keepdims=True) + EPS) * lg_ref[0] + lb_ref[0]
    conv_out = _mm(jax.nn.silu(ln).astype(_BF16), wco_ref[0]) + bco_ref[0]
    merged = gc_ref[...].astype(_F32) * conv_out + ga_ref[...].astype(_F32) * attn_out
    x = x_ref[...] + gate_m * _mm(merged.astype(_BF16), wout_ref[0])

    hb = ((_rms(x) * g_ref[0]) * (1.0 + scale) + shift).astype(_BF16)
    for c0 in range(0, D_FF, FFN_CHUNK):
        cw = min(FFN_CHUNK, D_FF - c0)
        gg = _mm(hb, w1_ref[0, :, c0:c0 + cw])
        uu = _mm(hb, w1_ref[0, :, D_FF + c0:D_FF + c0 + cw])
        h2 = (jax.nn.silu(gg) * uu).astype(_BF16)
        part = _mm(h2, w2_ref[0, c0:c0 + cw, :])
        if c0 == 0:
            acc_ref[...] = part
        else:
            acc_ref[...] += part
    y = x + gate_f * acc_ref[...]
    if final:
        y = _rms(y) * fg_ref[...]
    out_ref[...] = y


def _merge_ffn_call(l, x, mod, y, o, gc, ga, lg, lb, wco, bco, wao, wout, g, w1, w2, fg, seq,
                    final):
    tokens = x.shape[0]
    tm = TM_FFN
    row = lambda i: (i, 0)
    return pl.pallas_call(
        functools.partial(_merge_ffn_kernel, final=final),
        grid=(tokens // tm,),
        in_specs=[
            pl.BlockSpec((tm, D_MODEL), row),
            _mod_spec(l, seq // tm),
            pl.BlockSpec((tm, CONV_WIDTH), row),
            pl.BlockSpec((tm, ATTN_WIDTH), row),
            pl.BlockSpec((tm, D_MODEL), row),
            pl.BlockSpec((tm, D_MODEL), row),
            _layer((1, CONV_WIDTH), l),
            _layer((1, CONV_WIDTH), l),
            _layer((CONV_WIDTH, D_MODEL), l),
            _layer((1, D_MODEL), l),
            _layer((ATTN_WIDTH, D_MODEL), l),
            _layer((D_MODEL, D_MODEL), l),
            _layer((1, D_MODEL), l),
            _layer((D_MODEL, 2 * D_FF), 0),
            _layer((D_FF, D_MODEL), l),
            _resident((1, D_MODEL)),
        ],
        out_specs=pl.BlockSpec((tm, D_MODEL), row),
        out_shape=jax.ShapeDtypeStruct((tokens, D_MODEL), _F32),
        scratch_shapes=[pltpu.VMEM((tm, D_MODEL), _F32)],
        compiler_params=_params(1),
        name="merge_ffn_final" if final else "merge_ffn",
    )(x, mod, y, o, gc, ga, lg, lb, wco, bco, wao, wout, g, w1, w2, fg)


def _rope_tables(seq):
    rows = seq // GRID_W
    row_pos = jnp.broadcast_to(jnp.arange(rows)[:, None], (rows, GRID_W)).reshape(-1).astype(_F32)
    col_pos = jnp.broadcast_to(jnp.arange(GRID_W)[None, :], (rows, GRID_W)).reshape(-1).astype(_F32)
    inv_freq = ROPE_THETA ** (-jnp.arange(0, ROPE_AXIS_DIM, 2, dtype=_F32) / ROPE_AXIS_DIM)
    ang_r = row_pos[:, None] * inv_freq[None, :]
    ang_c = col_pos[:, None] * inv_freq[None, :]
    cos_r, sin_r, cos_c, sin_c = jnp.cos(ang_r), jnp.sin(ang_r), jnp.cos(ang_c), jnp.sin(ang_c)
    zero = jnp.zeros_like(sin_r)
    cos = jnp.concatenate([cos_r, cos_r, cos_c, cos_c], axis=-1)
    sin_a = jnp.concatenate([-sin_r, zero, -sin_c, zero], axis=-1)
    sin_b = jnp.concatenate([zero, sin_r, zero, sin_c], axis=-1)
    reps = LANES // HEAD_DIM
    return jnp.tile(cos, (1, reps)), jnp.tile(sin_a, (1, reps)), jnp.tile(sin_b, (1, reps))


def _head_block_diag():
    head = jnp.arange(GROUP_WIDTH) // HEAD_DIM
    return (head[:, None] == head[None, :]).astype(_BF16)


def kernel(x, c, w_ada, b_ada, norm_mix_g, w_in, q_norm_g, k_norm_g, w_attn_o, conv_dw, conv_dw_b,
           conv_ln_g, conv_ln_b, w_conv_o, b_conv_o, w_out, norm_ffn_g, w_ffn_in, w_ffn_out,
           final_norm_g):
    batch, seq, d = x.shape
    depth = w_in.shape[0]
    assert d == D_MODEL and seq % TM_MIX == 0 and seq % TM_FFN == 0
    assert seq % TQ_STEP == 0 and TQ_STEP % TQ_SUB == 0 and seq % GRID_W == 0

    cos, sin_a, sin_b = _rope_tables(seq)
    bd = _head_block_diag()
    mod = _ada_call(c, w_ada, b_ada).reshape(depth, batch, N_MOD, D_MODEL)
    vec = lambda a: a.reshape(depth, 1, -1)
    reps = LANES // HEAD_DIM
    gq, gk = vec(jnp.tile(q_norm_g, (1, reps))), vec(jnp.tile(k_norm_g, (1, reps)))
    g_mix, g_ffn, g_final = vec(norm_mix_g), vec(norm_ffn_g), final_norm_g.reshape(1, -1)
    cb, lg, lb, bco = vec(conv_dw_b), vec(conv_ln_g), vec(conv_ln_b), vec(b_conv_o)

    xt = x.reshape(batch * seq, d)
    for l in range(depth):
        q, kexp, vaug, u, gc, ga, w_ffn_in_b = _mix_in_call(l, xt, mod, g_mix, w_in, bd, gq, gk,
                                                            cos, sin_a, sin_b, w_ffn_in, seq)
        o, y = _attn_call(l, q, kexp, vaug, u, conv_dw, cb, batch, seq)
        xt = _merge_ffn_call(l, xt, mod, y, o, gc, ga, lg, lb, w_conv_o, bco, w_attn_o, w_out,
                             g_ffn, w_ffn_in_b, w_ffn_out, g_final, seq, final=(l == depth - 1))
    return xt.reshape(batch, seq, d)
```

```python
import functools
import math

import jax
import jax.numpy as jnp
from jax import lax
from jax.experimental import pallas as pl
from jax.experimental.pallas import tpu as pltpu

D_MODEL = 1024
GRID_W = 64
N_Q_HEADS = 8
N_KV_HEADS = 2
GROUP = N_Q_HEADS // N_KV_HEADS
HEAD_DIM = 64
ATTN_WIDTH = N_Q_HEADS * HEAD_DIM
KV_WIDTH = N_KV_HEADS * HEAD_DIM
CONV_WIDTH = D_MODEL // 2
CONV_KERNEL = 31
CONV_PAD = CONV_KERNEL // 2
ROPE_THETA = 10000.0
ROPE_AXIS_DIM = HEAD_DIM // 2
D_FF = 2816
IN_WIDTH = ATTN_WIDTH + 2 * KV_WIDTH + 2 * CONV_WIDTH + 2 * D_MODEL
EPS = 1e-6
N_MOD = 6

_OFF_K = ATTN_WIDTH
_OFF_V = _OFF_K + KV_WIDTH
_OFF_GLU_A = _OFF_V + KV_WIDTH
_OFF_GLU_B = _OFF_GLU_A + CONV_WIDTH
_OFF_GATE_CONV = _OFF_GLU_B + CONV_WIDTH
_OFF_GATE_ATTN = _OFF_GATE_CONV + D_MODEL

LANES = 128
SUBLANES = 8
HALO = 16
GROUP_WIDTH = GROUP * HEAD_DIM

TM_MIX = 512
TQ_STEP = 512
TQ_SUB = 256
KEY_TILE = 256
PV_CHUNK = 512
PV_ROW_SPLIT = 2
ANCHOR_LAG = 1
TM_FFN = 512
TN_ADA = 3072
CONV_ROWS = 64
FFN_CHUNK = 512
VMEM_LIMIT = 56 * 1024 * 1024

_Q_SCALE = HEAD_DIM ** -0.5 * math.log2(math.e)

_F32 = jnp.float32
_BF16 = jnp.bfloat16


def _resident(shape):
    nd = len(shape)
    return pl.BlockSpec(shape, lambda *_: (0,) * nd, pipeline_mode=pl.Buffered(1))


def _layer(shape, l):
    nd = len(shape)
    return pl.BlockSpec((1,) + shape, lambda *_: (l,) + (0,) * nd, pipeline_mode=pl.Buffered(1))


def _mod_spec(l, batch):
    return pl.BlockSpec((1, batch, N_MOD * D_MODEL), lambda *_: (l, 0, 0),
                        pipeline_mode=pl.Buffered(1))


def _mod_rows(mod_ref, tiles_per_seq, which):
    row = mod_ref[0, pl.ds(pl.program_id(0) // tiles_per_seq, 1), :]
    return [row[:, k * D_MODEL:(k + 1) * D_MODEL] for k in which]


def _params(n_axes, flags=None):
    return pltpu.CompilerParams(dimension_semantics=("arbitrary",) * n_axes,
                                vmem_limit_bytes=VMEM_LIMIT, flags=flags)


def _mm(a, w):
    return jnp.dot(a, w.astype(_BF16), preferred_element_type=_F32)


def _rms(x):
    return x * lax.rsqrt(jnp.mean(x * x, axis=-1, keepdims=True) + EPS)


def _split_bf16(a):
    hi = a.astype(_BF16)
    return hi, (a - hi.astype(_F32)).astype(_BF16)


def _ada_kernel(c_ref, w_ref, b_ref, o_ref):
    batch = c_ref.shape[0]
    c_hi, c_lo = _split_bf16(jax.nn.silu(c_ref[...]))
    w_hi, w_lo = _split_bf16(w_ref[0])
    both = jnp.dot(jnp.concatenate([c_hi, c_lo], axis=0), w_hi, preferred_element_type=_F32)
    o_ref[0] = (both[:batch] + both[batch:] + jnp.dot(c_hi, w_lo, preferred_element_type=_F32)
                + b_ref[pl.ds(pl.program_id(0), 1), :])


def _ada_call(c, w_ada, b_ada):
    depth, _, width = w_ada.shape
    batch = c.shape[0]
    return pl.pallas_call(
        _ada_kernel,
        grid=(depth, width // TN_ADA),
        in_specs=[
            pl.BlockSpec((batch, D_MODEL), lambda l, j: (0, 0)),
            pl.BlockSpec((1, D_MODEL, TN_ADA), lambda l, j: (l, 0, j)),
            pl.BlockSpec((depth, TN_ADA), lambda l, j: (0, j)),
        ],
        out_specs=pl.BlockSpec((1, batch, TN_ADA), lambda l, j: (l, 0, j)),
        out_shape=jax.ShapeDtypeStruct((depth, batch, width), _F32),
        compiler_params=_params(2),
        name="ada_mod",
    )(c, w_ada, b_ada)


def _head_norm_rope(z, bd, g, cos, sa, sb):
    sq = (z * z).astype(_BF16)
    wd = bd.shape[0]
    ms = jnp.concatenate([jnp.dot(sq[:, c0:c0 + wd], bd, preferred_element_type=_F32)
                          for c0 in range(0, z.shape[1], wd)], axis=1) * (1.0 / HEAD_DIM)
    zn = z * lax.rsqrt(ms + EPS)
    out = []
    for j in range(z.shape[1] // LANES):
        c = zn[:, j * LANES:(j + 1) * LANES] * g
        out.append(c * cos + pltpu.roll(c, LANES - ROPE_AXIS_DIM // 2, 1) * sa
                   + pltpu.roll(c, ROPE_AXIS_DIM // 2, 1) * sb)
    return out


def _mix_in_kernel(x_ref, mod_ref, g_ref, w_ref, bd_ref, gq_ref, gk_ref, cos_ref, sa_ref, sb_ref,
                   wf_ref, q_ref, kexp_ref, vaug_ref, u_ref, gc_ref, ga_ref, wfb_ref,
                   *, l, tiles_per_seq):
    wfb_ref[0] = wf_ref[0].astype(wfb_ref.dtype)

    shift, scale = _mod_rows(mod_ref, tiles_per_seq, (0, 1))
    h = (_rms(x_ref[...]) * g_ref[l:l + 1]) * (1.0 + scale) + shift
    hb = h.astype(_BF16)

    def proj(off, width):
        return _mm(hb, w_ref[0, :, off:off + width])

    zq, zk = proj(0, ATTN_WIDTH), proj(_OFF_K, KV_WIDTH)
    gc_ref[...] = jax.nn.sigmoid(proj(_OFF_GATE_CONV, D_MODEL)).astype(gc_ref.dtype)

    cos, sa, sb = cos_ref[...], sa_ref[...], sb_ref[...]
    q = _head_norm_rope(zq, bd_ref[...], gq_ref[l:l + 1], cos, sa, sb)
    for j, c in enumerate(q):
        q_ref[:, j * LANES:(j + 1) * LANES] = (c * _Q_SCALE).astype(q_ref.dtype)
    k, = _head_norm_rope(zk, bd_ref[0:KV_WIDTH, 0:KV_WIDTH], gk_ref[l:l + 1], cos, sa, sb)

    ga_ref[...] = jax.nn.sigmoid(proj(_OFF_GATE_ATTN, D_MODEL)).astype(ga_ref.dtype)
    u_ref[...] = proj(_OFF_GLU_A, CONV_WIDTH) * jax.nn.sigmoid(proj(_OFF_GLU_B, CONV_WIDTH))

    v = proj(_OFF_V, KV_WIDTH)
    low = lax.broadcasted_iota(jnp.int32, k.shape, 1) < HEAD_DIM
    k_sw, v_sw = pltpu.roll(k, HEAD_DIM, 1), pltpu.roll(v, HEAD_DIM, 1)
    for hk, (k2, v1) in enumerate(((jnp.where(low, k, k_sw), v), (jnp.where(low, k_sw, k), v_sw))):
        k2 = k2.astype(kexp_ref.dtype)
        for rep in range(GROUP_WIDTH // LANES):
            off = hk * GROUP_WIDTH + rep * LANES
            kexp_ref[:, off:off + LANES] = k2
        vaug_ref[:, hk * LANES:(hk + 1) * LANES] = jnp.where(low, v1, 1.0).astype(vaug_ref.dtype)


def _mix_in_call(l, x, mod, g, w_in, bd, gq, gk, cos, sa, sb, w_ffn_in, seq):
    tokens = x.shape[0]
    steps = tokens // TM_MIX
    tiles_per_seq = seq // TM_MIX
    wf_rows = D_MODEL // steps
    row = lambda i: (i, 0)
    pos = lambda i: (i % tiles_per_seq, 0)
    widths = (ATTN_WIDTH, N_KV_HEADS * GROUP_WIDTH, N_KV_HEADS * LANES, CONV_WIDTH, D_MODEL, D_MODEL)
    dtypes = (_BF16, _BF16, _BF16, _F32, _BF16, _BF16)
    return pl.pallas_call(
        functools.partial(_mix_in_kernel, l=l, tiles_per_seq=tiles_per_seq),
        grid=(steps,),
        in_specs=[
            pl.BlockSpec((TM_MIX, D_MODEL), row),
            _mod_spec(l, mod.shape[1]),
            _resident(g.shape),
            _layer((D_MODEL, IN_WIDTH), l),
            _resident((GROUP_WIDTH, GROUP_WIDTH)),
            _resident(gq.shape),
            _resident(gk.shape),
            pl.BlockSpec((TM_MIX, LANES), pos),
            pl.BlockSpec((TM_MIX, LANES), pos),
            pl.BlockSpec((TM_MIX, LANES), pos),
            pl.BlockSpec((1, wf_rows, 2 * D_FF), lambda i: (l, i, 0)),
        ],
        out_specs=[pl.BlockSpec((TM_MIX, w), row) for w in widths]
        + [pl.BlockSpec((1, wf_rows, 2 * D_FF), lambda i: (0, i, 0))],
        out_shape=[jax.ShapeDtypeStruct((tokens, w), dt) for w, dt in zip(widths, dtypes)]
        + [jax.ShapeDtypeStruct((1, D_MODEL, 2 * D_FF), _BF16)],
        compiler_params=_params(1),
        name="mix_in",
    )(x, mod, g, w_in, bd, gq, gk, cos, sa, sb, w_ffn_in)


def _conv_pieces(l, uc_ref, up_ref, un_ref, cw_ref, cb_ref, y_ref, win_ref, sh_ref, first, last):
    tm = uc_ref.shape[0]

    def window():
        win_ref[0:HALO] = jnp.where(first, 0.0, up_ref[...])
        win_ref[HALO:HALO + tm] = uc_ref[...]
        win_ref[HALO + tm:] = jnp.where(last, 0.0, un_ref[...])
        sh_rows = sh_ref.shape[1]
        for r in range(1, SUBLANES):
            sh_ref[r - 1] = win_ref[r:r + sh_rows, :]

    def block(c, rb):
        lanes = slice(c * LANES, (c + 1) * LANES)
        acc = jnp.broadcast_to(cb_ref[l:l + 1, lanes], (CONV_ROWS, LANES))
        for k in range(CONV_KERNEL):
            a, r = divmod(HALO - CONV_PAD + k, SUBLANES)
            rows = slice(rb + a * SUBLANES, rb + a * SUBLANES + CONV_ROWS)
            tap = win_ref[rows, lanes] if r == 0 else sh_ref[r - 1, rows, lanes]
            acc = acc + tap * cw_ref[0, k:k + 1, lanes]
        y_ref[rb:rb + CONV_ROWS, lanes] = acc
        word = lax.bitcast_convert_type(acc, jnp.int32)
        bits = functools.reduce(jnp.bitwise_or, [word[i:i + SUBLANES]
                                                 for i in range(0, CONV_ROWS, SUBLANES)])
        return lax.shift_right_logical(lax.shift_right_logical(bits, 16), 16)

    blocks = [functools.partial(block, c, rb)
              for c in range(CONV_WIDTH // LANES) for rb in range(0, tm, CONV_ROWS)]
    return [window] + blocks


def _spread(items, n):
    bounds = [round(i * len(items) / n) for i in range(n + 1)]
    return [items[bounds[i]:bounds[i + 1]] for i in range(n)]


def _attn_kernel(q_ref, kexp_ref, vaug_ref, uc_ref, up_ref, un_ref, cw_ref, cb_ref,
                 o_ref, y_ref, s_ref, p_ref, acc_ref, win_ref, sh_ref, *, l, conv_tiles_per_seq):
    seq = kexp_ref.shape[0]
    rows = GROUP * TQ_SUB
    assert TQ_STEP == 2 * TQ_SUB
    t = pl.program_id(1) * pl.num_programs(2) + pl.program_id(2)
    conv = _conv_pieces(l, uc_ref, up_ref, un_ref, cw_ref, cb_ref, y_ref, win_ref, sh_ref,
                        t == 0, t == conv_tiles_per_seq - 1)
    lane_head = lax.broadcasted_iota(jnp.int32, (TQ_SUB, GROUP_WIDTH), 1) // HEAD_DIM
    low = lax.broadcasted_iota(jnp.int32, (TQ_SUB, LANES), 1) < HEAD_DIM

    def stacked(sub):
        q = q_ref[sub * TQ_SUB:(sub + 1) * TQ_SUB, :]
        return jnp.concatenate(
            [jnp.where(lane_head == g, q, jnp.zeros_like(q)) for g in range(GROUP)], axis=0)

    def score_tile(sub, stack, j, after=None):
        keys = slice(j * KEY_TILE, (j + 1) * KEY_TILE)
        k_tile = kexp_ref[keys, :]
        if after is not None:
            zero = jnp.tile(after.astype(_F32), (2, 1)).astype(_BF16)
            k_tile = k_tile + jnp.tile(zero, (KEY_TILE // (2 * SUBLANES), GROUP_WIDTH // LANES))
        s_ref[sub, :, keys] = lax.dot_general(stack, k_tile, (((1,), (1,)), ((), ())),
                                              preferred_element_type=_F32)

    def softmax_rows(sub, j, n):
        r = slice(j * rows // n, (j + 1) * rows // n)
        s = s_ref[sub, r, :]
        p_ref[sub, r, :] = jnp.exp2(s - jnp.max(s, axis=-1, keepdims=True)).astype(_BF16)

    def pv_chunk(sub, half, j):
        r = slice(half * rows // PV_ROW_SPLIT, (half + 1) * rows // PV_ROW_SPLIT)
        keys = slice(j * PV_CHUNK, (j + 1) * PV_CHUNK)
        part = jnp.dot(p_ref[sub, r, keys], vaug_ref[keys, :], preferred_element_type=_F32)
        if j == 0:
            acc_ref[sub, r, :] = part
        else:
            acc_ref[sub, r, :] += part

    def write_out(sub):
        acc = acc_ref[sub]
        r = acc / pltpu.roll(acc, HEAD_DIM, 1)
        head = [r[g * TQ_SUB:(g + 1) * TQ_SUB] for g in range(GROUP)]
        out_rows = slice(sub * TQ_SUB, (sub + 1) * TQ_SUB)
        for pair in range(GROUP // 2):
            both = jnp.where(low, head[2 * pair], pltpu.roll(head[2 * pair + 1], HEAD_DIM, 1))
            o_ref[out_rows, pair * LANES:(pair + 1) * LANES] = both.astype(o_ref.dtype)

    n_score = seq // KEY_TILE
    n_pv = seq // PV_CHUNK

    stacks = [stacked(0), stacked(1)]
    tokens = []
    for j, pieces in enumerate(_spread(conv, 2 * n_score)):
        after = tokens[j - 1 - ANCHOR_LAG] if j > ANCHOR_LAG else None
        score_tile(j // n_score, stacks[j // n_score], j % n_score, after=after)
        done = [t for t in (piece() for piece in pieces) if t is not None]
        tokens.append(functools.reduce(jnp.bitwise_or, done) if done else None)
    for sub in range(2):
        for half, chunks in enumerate(_spread(list(range(n_score)), PV_ROW_SPLIT)):
            for i in chunks:
                softmax_rows(sub, i, n_score)
            for j in range(n_pv):
                pv_chunk(sub, half, j)
        write_out(sub)


def _conv_tile_specs(tm, tile, n_halo):
    halo_per_tile = tm // HALO
    return [
        pl.BlockSpec((tm, CONV_WIDTH), lambda *ids: (tile(*ids), 0)),
        pl.BlockSpec((HALO, CONV_WIDTH),
                     lambda *ids: (jnp.maximum(tile(*ids) * halo_per_tile - 1, 0), 0)),
        pl.BlockSpec((HALO, CONV_WIDTH),
                     lambda *ids: (jnp.minimum((tile(*ids) + 1) * halo_per_tile, n_halo - 1), 0)),
    ]


def _attn_call(l, q, kexp, vaug, u, cw, cb, batch, seq):
    tokens = q.shape[0]
    nq = seq // TQ_STEP
    conv_tiles_per_seq = N_KV_HEADS * nq
    tm = seq // conv_tiles_per_seq
    conv_tile = lambda b, h, i: b * conv_tiles_per_seq + h * nq + i
    sh_rows = tm + ((CONV_KERNEL + HALO - CONV_PAD - 1) // SUBLANES) * SUBLANES
    return pl.pallas_call(
        functools.partial(_attn_kernel, l=l, conv_tiles_per_seq=conv_tiles_per_seq),
        grid=(batch, N_KV_HEADS, nq),
        in_specs=[
            pl.BlockSpec((TQ_STEP, GROUP_WIDTH), lambda b, h, i: (b * nq + i, h)),
            pl.BlockSpec((seq, GROUP_WIDTH), lambda b, h, i: (b, h)),
            pl.BlockSpec((seq, LANES), lambda b, h, i: (b, h)),
            *_conv_tile_specs(tm, conv_tile, tokens // HALO),
            _layer((CONV_KERNEL, CONV_WIDTH), l),
            _resident(cb.shape),
        ],
        out_specs=[
            pl.BlockSpec((TQ_STEP, GROUP_WIDTH), lambda b, h, i: (b * nq + i, h)),
            pl.BlockSpec((tm, CONV_WIDTH), lambda b, h, i: (conv_tile(b, h, i), 0)),
        ],
        out_shape=[
            jax.ShapeDtypeStruct((tokens, ATTN_WIDTH), _BF16),
            jax.ShapeDtypeStruct((tokens, CONV_WIDTH), _F32),
        ],
        scratch_shapes=[
            pltpu.VMEM((2, GROUP * TQ_SUB, seq), _F32),
            pltpu.VMEM((2, GROUP * TQ_SUB, seq), _BF16),
            pltpu.VMEM((2, GROUP * TQ_SUB, LANES), _F32),
            pltpu.VMEM((tm + 2 * HALO, CONV_WIDTH), _F32),
            pltpu.VMEM((SUBLANES - 1, sh_rows, CONV_WIDTH), _F32),
        ],
        compiler_params=_params(3),
        name="attention",
    )(q, kexp, vaug, u, u, u, cw, cb)


def _merge_ffn_kernel(x_ref, mod_ref, y_ref, o_ref, gc_ref, ga_ref, lg_ref, lb_ref, wco_ref, bco_ref,
                      wao_ref, wout_ref, g_ref, w1_ref, w2_ref, fg_ref, out_ref, acc_ref,
                      *, l, tiles_per_seq, final):
    gate_m, shift, scale, gate_f = _mod_rows(mod_ref, tiles_per_seq, (2, 3, 4, 5))

    attn_out = _mm(o_ref[...], wao_ref[0])
    y = y_ref[...]
    yc = y - jnp.mean(y, axis=-1, keepdims=True)
    ln = (yc * lax.rsqrt(jnp.mean(yc * yc, axis=-1, keepdims=True) + EPS) * lg_ref[l:l + 1]
          + lb_ref[l:l + 1])
    conv_out = _mm(jax.nn.silu(ln).astype(_BF16), wco_ref[0]) + bco_ref[l:l + 1]
    merged = gc_ref[...].astype(_F32) * conv_out + ga_ref[...].astype(_F32) * attn_out
    x = x_ref[...] + gate_m * _mm(merged.astype(_BF16), wout_ref[0])

    hb = ((_rms(x) * g_ref[l:l + 1]) * (1.0 + scale) + shift).astype(_BF16)
    for c0 in range(0, D_FF, FFN_CHUNK):
        cw = min(FFN_CHUNK, D_FF - c0)
        gg = _mm(hb, w1_ref[0, :, c0:c0 + cw])
        uu = _mm(hb, w1_ref[0, :, D_FF + c0:D_FF + c0 + cw])
        h2 = (jax.nn.silu(gg) * uu).astype(_BF16)
        part = _mm(h2, w2_ref[0, c0:c0 + cw, :])
        if c0 == 0:
            acc_ref[...] = part
        else:
            acc_ref[...] += part
    y = x + gate_f * acc_ref[...]
    if final:
        y = _rms(y) * fg_ref[...]
    out_ref[...] = y


def _merge_ffn_call(l, x, mod, y, o, gc, ga, lg, lb, wco, bco, wao, wout, g, w1, w2, fg, seq,
                    final):
    tokens = x.shape[0]
    tm = TM_FFN
    row = lambda i: (i, 0)
    return pl.pallas_call(
        functools.partial(_merge_ffn_kernel, l=l, tiles_per_seq=seq // tm, final=final),
        grid=(tokens // tm,),
        in_specs=[
            pl.BlockSpec((tm, D_MODEL), row),
            _mod_spec(l, mod.shape[1]),
            pl.BlockSpec((tm, CONV_WIDTH), row),
            pl.BlockSpec((tm, ATTN_WIDTH), row),
            pl.BlockSpec((tm, D_MODEL), row),
            pl.BlockSpec((tm, D_MODEL), row),
            _resident(lg.shape),
            _resident(lb.shape),
            _layer((CONV_WIDTH, D_MODEL), l),
            _resident(bco.shape),
            _layer((ATTN_WIDTH, D_MODEL), l),
            _layer((D_MODEL, D_MODEL), l),
            _resident(g.shape),
            _layer((D_MODEL, 2 * D_FF), 0),
            _layer((D_FF, D_MODEL), l),
            _resident((1, D_MODEL)),
        ],
        out_specs=pl.BlockSpec((tm, D_MODEL), row),
        out_shape=jax.ShapeDtypeStruct((tokens, D_MODEL), _F32),
        scratch_shapes=[pltpu.VMEM((tm, D_MODEL), _F32)],
        compiler_params=_params(1),
        name="merge_ffn_final" if final else "merge_ffn",
    )(x, mod, y, o, gc, ga, lg, lb, wco, bco, wao, wout, g, w1, w2, fg)


def _rope_tables(seq):
    rows = seq // GRID_W
    row_pos = jnp.broadcast_to(jnp.arange(rows)[:, None], (rows, GRID_W)).reshape(-1).astype(_F32)
    col_pos = jnp.broadcast_to(jnp.arange(GRID_W)[None, :], (rows, GRID_W)).reshape(-1).astype(_F32)
    inv_freq = ROPE_THETA ** (-jnp.arange(0, ROPE_AXIS_DIM, 2, dtype=_F32) / ROPE_AXIS_DIM)
    ang_r = row_pos[:, None] * inv_freq[None, :]
    ang_c = col_pos[:, None] * inv_freq[None, :]
    cos_r, sin_r, cos_c, sin_c = jnp.cos(ang_r), jnp.sin(ang_r), jnp.cos(ang_c), jnp.sin(ang_c)
    zero = jnp.zeros_like(sin_r)
    cos = jnp.concatenate([cos_r, cos_r, cos_c, cos_c], axis=-1)
    sin_a = jnp.concatenate([-sin_r, zero, -sin_c, zero], axis=-1)
    sin_b = jnp.concatenate([zero, sin_r, zero, sin_c], axis=-1)
    reps = LANES // HEAD_DIM
    return jnp.tile(cos, (1, reps)), jnp.tile(sin_a, (1, reps)), jnp.tile(sin_b, (1, reps))


def _head_block_diag():
    head = jnp.arange(GROUP_WIDTH) // HEAD_DIM
    return (head[:, None] == head[None, :]).astype(_BF16)


def kernel(x, c, w_ada, b_ada, norm_mix_g, w_in, q_norm_g, k_norm_g, w_attn_o, conv_dw, conv_dw_b,
           conv_ln_g, conv_ln_b, w_conv_o, b_conv_o, w_out, norm_ffn_g, w_ffn_in, w_ffn_out,
           final_norm_g):
    batch, seq, d = x.shape
    depth = w_in.shape[0]
    assert d == D_MODEL and seq % TM_MIX == 0 and seq % TM_FFN == 0
    assert seq % TQ_STEP == 0 and TQ_STEP % TQ_SUB == 0 and seq % GRID_W == 0

    cos, sin_a, sin_b = _rope_tables(seq)
    bd = _head_block_diag()
    mod = _ada_call(c, w_ada, b_ada)
    reps = LANES // HEAD_DIM
    gq, gk = jnp.tile(q_norm_g, (1, reps)), jnp.tile(k_norm_g, (1, reps))
    g_mix, g_ffn, g_final = norm_mix_g, norm_ffn_g, final_norm_g.reshape(1, -1)
    cb, lg, lb, bco = conv_dw_b, conv_ln_g, conv_ln_b, b_conv_o

    xt = x.reshape(batch * seq, d)
    for l in range(depth):
        q, kexp, vaug, u, gc, ga, w_ffn_in_b = _mix_in_call(l, xt, mod, g_mix, w_in, bd, gq, gk,
                                                            cos, sin_a, sin_b, w_ffn_in, seq)
        o, y = _attn_call(l, q, kexp, vaug, u, conv_dw, cb, batch, seq)
        xt = _merge_ffn_call(l, xt, mod, y, o, gc, ga, lg, lb, w_conv_o, bco, w_attn_o, w_out,
                             g_ffn, w_ffn_in_b, w_ffn_out, g_final, seq, final=(l == depth - 1))
    return xt.reshape(batch, seq, d)
```

```python
import functools
import math

import jax
import jax.numpy as jnp
from jax import lax
from jax.experimental import pallas as pl
from jax.experimental.pallas import tpu as pltpu

D_MODEL = 1024
GRID_W = 64
N_Q_HEADS = 8
N_KV_HEADS = 2
GROUP = N_Q_HEADS // N_KV_HEADS
HEAD_DIM = 64
ATTN_WIDTH = N_Q_HEADS * HEAD_DIM
KV_WIDTH = N_KV_HEADS * HEAD_DIM
CONV_WIDTH = D_MODEL // 2
CONV_KERNEL = 31
CONV_PAD = CONV_KERNEL // 2
ROPE_THETA = 10000.0
ROPE_AXIS_DIM = HEAD_DIM // 2
D_FF = 2816
IN_WIDTH = ATTN_WIDTH + 2 * KV_WIDTH + 2 * CONV_WIDTH + 2 * D_MODEL
EPS = 1e-6
N_MOD = 6

_OFF_K = ATTN_WIDTH
_OFF_V = _OFF_K + KV_WIDTH
_OFF_GLU_A = _OFF_V + KV_WIDTH
_OFF_GLU_B = _OFF_GLU_A + CONV_WIDTH
_OFF_GATE_CONV = _OFF_GLU_B + CONV_WIDTH
_OFF_GATE_ATTN = _OFF_GATE_CONV + D_MODEL

LANES = 128
SUBLANES = 8
HALO = 16
GROUP_WIDTH = GROUP * HEAD_DIM

TM_MIX = 512
TQ_STEP = 512
TQ_SUB = 256
KEY_TILE = 256
PV_CHUNK = 512
PV_ROW_SPLIT = 2
ANCHOR_LAG = 1
TM_FFN = 512
TN_ADA = 3072
CONV_ROWS = 64
FFN_CHUNK = 256
VMEM_LIMIT = 56 * 1024 * 1024

_Q_SCALE = HEAD_DIM ** -0.5 * math.log2(math.e)

_F32 = jnp.float32
_BF16 = jnp.bfloat16


def _resident(shape):
    nd = len(shape)
    return pl.BlockSpec(shape, lambda *_: (0,) * nd, pipeline_mode=pl.Buffered(1))


def _layer(shape, l):
    nd = len(shape)
    return pl.BlockSpec((1,) + shape, lambda *_: (l,) + (0,) * nd, pipeline_mode=pl.Buffered(1))


def _mod_spec(l, batch):
    return pl.BlockSpec((1, batch, N_MOD * D_MODEL), lambda *_: (l, 0, 0),
                        pipeline_mode=pl.Buffered(1))


def _mod_rows(mod_ref, tiles_per_seq, which):
    row = mod_ref[0, pl.ds(pl.program_id(0) // tiles_per_seq, 1), :]
    return [row[:, k * D_MODEL:(k + 1) * D_MODEL] for k in which]


def _params(n_axes, flags=None):
    return pltpu.CompilerParams(dimension_semantics=("arbitrary",) * n_axes,
                                vmem_limit_bytes=VMEM_LIMIT, flags=flags)


def _mm(a, w):
    return jnp.dot(a, w.astype(_BF16), preferred_element_type=_F32)


def _rms(x):
    return x * lax.rsqrt(jnp.mean(x * x, axis=-1, keepdims=True) + EPS)


def _split_bf16(a):
    hi = a.astype(_BF16)
    return hi, (a - hi.astype(_F32)).astype(_BF16)


def _ada_kernel(c_ref, w_ref, b_ref, o_ref):
    batch = c_ref.shape[0]
    c_hi, c_lo = _split_bf16(jax.nn.silu(c_ref[...]))
    w_hi, w_lo = _split_bf16(w_ref[0])
    both = jnp.dot(jnp.concatenate([c_hi, c_lo], axis=0), w_hi, preferred_element_type=_F32)
    o_ref[0] = (both[:batch] + both[batch:] + jnp.dot(c_hi, w_lo, preferred_element_type=_F32)
                + b_ref[pl.ds(pl.program_id(0), 1), :])


def _ada_call(c, w_ada, b_ada):
    depth, _, width = w_ada.shape
    batch = c.shape[0]
    return pl.pallas_call(
        _ada_kernel,
        grid=(depth, width // TN_ADA),
        in_specs=[
            pl.BlockSpec((batch, D_MODEL), lambda l, j: (0, 0)),
            pl.BlockSpec((1, D_MODEL, TN_ADA), lambda l, j: (l, 0, j)),
            pl.BlockSpec((depth, TN_ADA), lambda l, j: (0, j)),
        ],
        out_specs=pl.BlockSpec((1, batch, TN_ADA), lambda l, j: (l, 0, j)),
        out_shape=jax.ShapeDtypeStruct((depth, batch, width), _F32),
        compiler_params=_params(2),
        name="ada_mod",
    )(c, w_ada, b_ada)


def _head_norm_rope(z, bd, g, cos, sa, sb):
    sq = (z * z).astype(_BF16)
    wd = bd.shape[0]
    ms = jnp.concatenate([jnp.dot(sq[:, c0:c0 + wd], bd, preferred_element_type=_F32)
                          for c0 in range(0, z.shape[1], wd)], axis=1) * (1.0 / HEAD_DIM)
    zn = z * lax.rsqrt(ms + EPS)
    out = []
    for j in range(z.shape[1] // LANES):
        c = zn[:, j * LANES:(j + 1) * LANES] * g
        out.append(c * cos + pltpu.roll(c, LANES - ROPE_AXIS_DIM // 2, 1) * sa
                   + pltpu.roll(c, ROPE_AXIS_DIM // 2, 1) * sb)
    return out


def _mix_in_kernel(x_ref, mod_ref, g_ref, w_ref, bd_ref, gq_ref, gk_ref, cos_ref, sa_ref, sb_ref,
                   wf_ref, q_ref, kexp_ref, vaug_ref, u_ref, gc_ref, ga_ref, wfb_ref,
                   *, l, tiles_per_seq):
    wfb_ref[0] = wf_ref[0].astype(wfb_ref.dtype)

    shift, scale = _mod_rows(mod_ref, tiles_per_seq, (0, 1))
    h = (_rms(x_ref[...]) * g_ref[l:l + 1]) * (1.0 + scale) + shift
    hb = h.astype(_BF16)

    def proj(off, width):
        return _mm(hb, w_ref[0, :, off:off + width])

    zq, zkv = proj(0, ATTN_WIDTH), proj(_OFF_K, 2 * KV_WIDTH)
    zk, v = zkv[:, :KV_WIDTH], zkv[:, KV_WIDTH:]
    gc_ref[...] = jax.nn.sigmoid(proj(_OFF_GATE_CONV, D_MODEL)).astype(gc_ref.dtype)

    cos, sa, sb = cos_ref[...], sa_ref[...], sb_ref[...]
    q = _head_norm_rope(zq, bd_ref[...], gq_ref[l:l + 1], cos, sa, sb)
    for j, c in enumerate(q):
        q_ref[:, j * LANES:(j + 1) * LANES] = (c * _Q_SCALE).astype(q_ref.dtype)
    k, = _head_norm_rope(zk, bd_ref[0:KV_WIDTH, 0:KV_WIDTH], gk_ref[l:l + 1], cos, sa, sb)

    ga_ref[...] = jax.nn.sigmoid(proj(_OFF_GATE_ATTN, D_MODEL)).astype(ga_ref.dtype)
    u_ref[...] = proj(_OFF_GLU_A, CONV_WIDTH) * jax.nn.sigmoid(proj(_OFF_GLU_B, CONV_WIDTH))

    low = lax.broadcasted_iota(jnp.int32, k.shape, 1) < HEAD_DIM
    k_sw, v_sw = pltpu.roll(k, HEAD_DIM, 1), pltpu.roll(v, HEAD_DIM, 1)
    for hk, (k2, v1) in enumerate(((jnp.where(low, k, k_sw), v), (jnp.where(low, k_sw, k), v_sw))):
        k2 = k2.astype(kexp_ref.dtype)
        for rep in range(GROUP_WIDTH // LANES):
            off = hk * GROUP_WIDTH + rep * LANES
            kexp_ref[:, off:off + LANES] = k2
        vaug_ref[:, hk * LANES:(hk + 1) * LANES] = jnp.where(low, v1, 1.0).astype(vaug_ref.dtype)


def _mix_in_call(l, x, mod, g, w_in, bd, gq, gk, cos, sa, sb, w_ffn_in, seq):
    tokens = x.shape[0]
    steps = tokens // TM_MIX
    tiles_per_seq = seq // TM_MIX
    wf_rows = D_MODEL // steps
    row = lambda i: (i, 0)
    pos = lambda i: (i % tiles_per_seq, 0)
    widths = (ATTN_WIDTH, N_KV_HEADS * GROUP_WIDTH, N_KV_HEADS * LANES, CONV_WIDTH, D_MODEL, D_MODEL)
    dtypes = (_BF16, _BF16, _BF16, _F32, _BF16, _BF16)
    return pl.pallas_call(
        functools.partial(_mix_in_kernel, l=l, tiles_per_seq=tiles_per_seq),
        grid=(steps,),
        in_specs=[
            pl.BlockSpec((TM_MIX, D_MODEL), row),
            _mod_spec(l, mod.shape[1]),
            _resident(g.shape),
            _layer((D_MODEL, IN_WIDTH), l),
            _resident((GROUP_WIDTH, GROUP_WIDTH)),
            _resident(gq.shape),
            _resident(gk.shape),
            pl.BlockSpec((TM_MIX, LANES), pos),
            pl.BlockSpec((TM_MIX, LANES), pos),
            pl.BlockSpec((TM_MIX, LANES), pos),
            pl.BlockSpec((1, wf_rows, 2 * D_FF), lambda i: (l, i, 0)),
        ],
        out_specs=[pl.BlockSpec((TM_MIX, w), row) for w in widths]
        + [pl.BlockSpec((1, wf_rows, 2 * D_FF), lambda i: (0, i, 0))],
        out_shape=[jax.ShapeDtypeStruct((tokens, w), dt) for w, dt in zip(widths, dtypes)]
        + [jax.ShapeDtypeStruct((1, D_MODEL, 2 * D_FF), _BF16)],
        compiler_params=_params(1),
        name="mix_in",
    )(x, mod, g, w_in, bd, gq, gk, cos, sa, sb, w_ffn_in)


def _conv_pieces(l, uc_ref, up_ref, un_ref, cw_ref, cb_ref, y_ref, win_ref, sh_ref, first, last):
    tm = uc_ref.shape[0]

    def window():
        win_ref[0:HALO] = jnp.where(first, 0.0, up_ref[...])
        win_ref[HALO:HALO + tm] = uc_ref[...]
        win_ref[HALO + tm:] = jnp.where(last, 0.0, un_ref[...])
        sh_rows = sh_ref.shape[1]
        for r in range(1, SUBLANES):
            sh_ref[r - 1] = win_ref[r:r + sh_rows, :]

    def block(c, rb):
        lanes = slice(c * LANES, (c + 1) * LANES)
        acc = jnp.broadcast_to(cb_ref[l:l + 1, lanes], (CONV_ROWS, LANES))
        for k in range(CONV_KERNEL):
            a, r = divmod(HALO - CONV_PAD + k, SUBLANES)
            rows = slice(rb + a * SUBLANES, rb + a * SUBLANES + CONV_ROWS)
            tap = win_ref[rows, lanes] if r == 0 else sh_ref[r - 1, rows, lanes]
            acc = acc + tap * cw_ref[0, k:k + 1, lanes]
        y_ref[rb:rb + CONV_ROWS, lanes] = acc
        word = lax.bitcast_convert_type(acc, jnp.int32)
        bits = functools.reduce(jnp.bitwise_or, [word[i:i + SUBLANES]
                                                 for i in range(0, CONV_ROWS, SUBLANES)])
        return lax.shift_right_logical(lax.shift_right_logical(bits, 16), 16)

    blocks = [functools.partial(block, c, rb)
              for c in range(CONV_WIDTH // LANES) for rb in range(0, tm, CONV_ROWS)]
    return [window] + blocks


def _spread(items, n):
    bounds = [round(i * len(items) / n) for i in range(n + 1)]
    return [items[bounds[i]:bounds[i + 1]] for i in range(n)]


def _attn_kernel(q_ref, kexp_ref, vaug_ref, uc_ref, up_ref, un_ref, cw_ref, cb_ref,
                 o_ref, y_ref, s_ref, p_ref, acc_ref, win_ref, sh_ref, *, l, conv_tiles_per_seq):
    seq = kexp_ref.shape[0]
    rows = GROUP * TQ_SUB
    assert TQ_STEP == 2 * TQ_SUB
    t = pl.program_id(1) * pl.num_programs(2) + pl.program_id(2)
    conv = _conv_pieces(l, uc_ref, up_ref, un_ref, cw_ref, cb_ref, y_ref, win_ref, sh_ref,
                        t == 0, t == conv_tiles_per_seq - 1)
    lane_head = lax.broadcasted_iota(jnp.int32, (TQ_SUB, GROUP_WIDTH), 1) // HEAD_DIM
    low = lax.broadcasted_iota(jnp.int32, (TQ_SUB, LANES), 1) < HEAD_DIM

    def stacked(sub):
        q = q_ref[sub * TQ_SUB:(sub + 1) * TQ_SUB, :]
        return jnp.concatenate(
            [jnp.where(lane_head == g, q, jnp.zeros_like(q)) for g in range(GROUP)], axis=0)

    def score_tile(sub, stack, j, after=None):
        keys = slice(j * KEY_TILE, (j + 1) * KEY_TILE)
        k_tile = kexp_ref[keys, :]
        if after is not None:
            zero = jnp.tile(after.astype(_F32), (2, 1)).astype(_BF16)
            k_tile = k_tile + jnp.tile(zero, (KEY_TILE // (2 * SUBLANES), GROUP_WIDTH // LANES))
        s_ref[sub, :, keys] = lax.dot_general(stack, k_tile, (((1,), (1,)), ((), ())),
                                              preferred_element_type=_F32)

    def softmax_rows(sub, j, n):
        r = slice(j * rows // n, (j + 1) * rows // n)
        s = s_ref[sub, r, :]
        p_ref[sub, r, :] = jnp.exp2(s - jnp.max(s, axis=-1, keepdims=True)).astype(_BF16)

    def pv_chunk(sub, half, j):
        r = slice(half * rows // PV_ROW_SPLIT, (half + 1) * rows // PV_ROW_SPLIT)
        keys = slice(j * PV_CHUNK, (j + 1) * PV_CHUNK)
        part = jnp.dot(p_ref[sub, r, keys], vaug_ref[keys, :], preferred_element_type=_F32)
        if j == 0:
            acc_ref[sub, r, :] = part
        else:
            acc_ref[sub, r, :] += part

    def write_out(sub):
        acc = acc_ref[sub]
        r = acc / pltpu.roll(acc, HEAD_DIM, 1)
        head = [r[g * TQ_SUB:(g + 1) * TQ_SUB] for g in range(GROUP)]
        out_rows = slice(sub * TQ_SUB, (sub + 1) * TQ_SUB)
        for pair in range(GROUP // 2):
            both = jnp.where(low, head[2 * pair], pltpu.roll(head[2 * pair + 1], HEAD_DIM, 1))
            o_ref[out_rows, pair * LANES:(pair + 1) * LANES] = both.astype(o_ref.dtype)

    n_score = seq // KEY_TILE
    n_pv = seq // PV_CHUNK

    stacks = [stacked(0), stacked(1)]
    tokens = []
    for j, pieces in enumerate(_spread(conv, 2 * n_score)):
        after = tokens[j - 1 - ANCHOR_LAG] if j > ANCHOR_LAG else None
        score_tile(j // n_score, stacks[j // n_score], j % n_score, after=after)
        done = [t for t in (piece() for piece in pieces) if t is not None]
        tokens.append(functools.reduce(jnp.bitwise_or, done) if done else None)
    for sub in range(2):
        for half, chunks in enumerate(_spread(list(range(n_score)), PV_ROW_SPLIT)):
            for i in chunks:
                softmax_rows(sub, i, n_score)
            for j in range(n_pv):
                pv_chunk(sub, half, j)
        write_out(sub)


def _conv_tile_specs(tm, tile, n_halo):
    halo_per_tile = tm // HALO
    return [
        pl.BlockSpec((tm, CONV_WIDTH), lambda *ids: (tile(*ids), 0)),
        pl.BlockSpec((HALO, CONV_WIDTH),
                     lambda *ids: (jnp.maximum(tile(*ids) * halo_per_tile - 1, 0), 0)),
        pl.BlockSpec((HALO, CONV_WIDTH),
                     lambda *ids: (jnp.minimum((tile(*ids) + 1) * halo_per_tile, n_halo - 1), 0)),
    ]


def _attn_call(l, q, kexp, vaug, u, cw, cb, batch, seq):
    tokens = q.shape[0]
    nq = seq // TQ_STEP
    conv_tiles_per_seq = N_KV_HEADS * nq
    tm = seq // conv_tiles_per_seq
    conv_tile = lambda b, h, i: b * conv_tiles_per_seq + h * nq + i
    sh_rows = tm + ((CONV_KERNEL + HALO - CONV_PAD - 1) // SUBLANES) * SUBLANES
    return pl.pallas_call(
        functools.partial(_attn_kernel, l=l, conv_tiles_per_seq=conv_tiles_per_seq),
        grid=(batch, N_KV_HEADS, nq),
        in_specs=[
            pl.BlockSpec((TQ_STEP, GROUP_WIDTH), lambda b, h, i: (b * nq + i, h)),
            pl.BlockSpec((seq, GROUP_WIDTH), lambda b, h, i: (b, h)),
            pl.BlockSpec((seq, LANES), lambda b, h, i: (b, h)),
            *_conv_tile_specs(tm, conv_tile, tokens // HALO),
            _layer((CONV_KERNEL, CONV_WIDTH), l),
            _resident(cb.shape),
        ],
        out_specs=[
            pl.BlockSpec((TQ_STEP, GROUP_WIDTH), lambda b, h, i: (b * nq + i, h)),
            pl.BlockSpec((tm, CONV_WIDTH), lambda b, h, i: (conv_tile(b, h, i), 0)),
        ],
        out_shape=[
            jax.ShapeDtypeStruct((tokens, ATTN_WIDTH), _BF16),
            jax.ShapeDtypeStruct((tokens, CONV_WIDTH), _F32),
        ],
        scratch_shapes=[
            pltpu.VMEM((2, GROUP * TQ_SUB, seq), _F32),
            pltpu.VMEM((2, GROUP * TQ_SUB, seq), _BF16),
            pltpu.VMEM((2, GROUP * TQ_SUB, LANES), _F32),
            pltpu.VMEM((tm + 2 * HALO, CONV_WIDTH), _F32),
            pltpu.VMEM((SUBLANES - 1, sh_rows, CONV_WIDTH), _F32),
        ],
        compiler_params=_params(3),
        name="attention",
    )(q, kexp, vaug, u, u, u, cw, cb)


def _merge_ffn_kernel(x_ref, mod_ref, y_ref, o_ref, gc_ref, ga_ref, lg_ref, lb_ref, wco_ref, bco_ref,
                      wao_ref, wout_ref, g_ref, w1_ref, w2_ref, fg_ref, out_ref, h2_ref,
                      *, l, tiles_per_seq, final):
    gate_m, shift, scale, gate_f = _mod_rows(mod_ref, tiles_per_seq, (2, 3, 4, 5))

    attn_out = _mm(o_ref[...], wao_ref[0])
    y = y_ref[...]
    yc = y - jnp.mean(y, axis=-1, keepdims=True)
    ln = (yc * lax.rsqrt(jnp.mean(yc * yc, axis=-1, keepdims=True) + EPS) * lg_ref[l:l + 1]
          + lb_ref[l:l + 1])
    conv_out = _mm(jax.nn.silu(ln).astype(_BF16), wco_ref[0]) + bco_ref[l:l + 1]
    merged = gc_ref[...].astype(_F32) * conv_out + ga_ref[...].astype(_F32) * attn_out
    x = x_ref[...] + gate_m * _mm(merged.astype(_BF16), wout_ref[0])

    hb = ((_rms(x) * g_ref[l:l + 1]) * (1.0 + scale) + shift).astype(_BF16)
    for c0 in range(0, D_FF, FFN_CHUNK):
        cw = min(FFN_CHUNK, D_FF - c0)
        gg = _mm(hb, w1_ref[0, :, c0:c0 + cw])
        uu = _mm(hb, w1_ref[0, :, D_FF + c0:D_FF + c0 + cw])
        h2_ref[:, c0:c0 + cw] = (jax.nn.silu(gg) * uu).astype(_BF16)
    y = x + gate_f * _mm(h2_ref[...], w2_ref[0])
    if final:
        y = _rms(y) * fg_ref[...]
    out_ref[...] = y


def _merge_ffn_call(l, x, mod, y, o, gc, ga, lg, lb, wco, bco, wao, wout, g, w1, w2, fg, seq,
                    final):
    tokens = x.shape[0]
    tm = TM_FFN
    row = lambda i: (i, 0)
    return pl.pallas_call(
        functools.partial(_merge_ffn_kernel, l=l, tiles_per_seq=seq // tm, final=final),
        grid=(tokens // tm,),
        in_specs=[
            pl.BlockSpec((tm, D_MODEL), row),
            _mod_spec(l, mod.shape[1]),
            pl.BlockSpec((tm, CONV_WIDTH), row),
            pl.BlockSpec((tm, ATTN_WIDTH), row),
            pl.BlockSpec((tm, D_MODEL), row),
            pl.BlockSpec((tm, D_MODEL), row),
            _resident(lg.shape),
            _resident(lb.shape),
            _layer((CONV_WIDTH, D_MODEL), l),
            _resident(bco.shape),
            _layer((ATTN_WIDTH, D_MODEL), l),
            _layer((D_MODEL, D_MODEL), l),
            _resident(g.shape),
            _layer((D_MODEL, 2 * D_FF), 0),
            _layer((D_FF, D_MODEL), l),
            _resident((1, D_MODEL)),
        ],
        out_specs=pl.BlockSpec((tm, D_MODEL), row),
        out_shape=jax.ShapeDtypeStruct((tokens, D_MODEL), _F32),
        scratch_shapes=[pltpu.VMEM((tm, D_FF), _BF16)],
        compiler_params=_params(1),
        name="merge_ffn_final" if final else "merge_ffn",
    )(x, mod, y, o, gc, ga, lg, lb, wco, bco, wao, wout, g, w1, w2, fg)


def _rope_tables(seq):
    rows = seq // GRID_W
    row_pos = jnp.broadcast_to(jnp.arange(rows)[:, None], (rows, GRID_W)).reshape(-1).astype(_F32)
    col_pos = jnp.broadcast_to(jnp.arange(GRID_W)[None, :], (rows, GRID_W)).reshape(-1).astype(_F32)
    inv_freq = ROPE_THETA ** (-jnp.arange(0, ROPE_AXIS_DIM, 2, dtype=_F32) / ROPE_AXIS_DIM)
    ang_r = row_pos[:, None] * inv_freq[None, :]
    ang_c = col_pos[:, None] * inv_freq[None, :]
    cos_r, sin_r, cos_c, sin_c = jnp.cos(ang_r), jnp.sin(ang_r), jnp.cos(ang_c), jnp.sin(ang_c)
    zero = jnp.zeros_like(sin_r)
    cos = jnp.concatenate([cos_r, cos_r, cos_c, cos_c], axis=-1)
    sin_a = jnp.concatenate([-sin_r, zero, -sin_c, zero], axis=-1)
    sin_b = jnp.concatenate([zero, sin_r, zero, sin_c], axis=-1)
    reps = LANES // HEAD_DIM
    return jnp.tile(cos, (1, reps)), jnp.tile(sin_a, (1, reps)), jnp.tile(sin_b, (1, reps))


def _head_block_diag():
    head = jnp.arange(GROUP_WIDTH) // HEAD_DIM
    return (head[:, None] == head[None, :]).astype(_BF16)


def kernel(x, c, w_ada, b_ada, norm_mix_g, w_in, q_norm_g, k_norm_g, w_attn_o, conv_dw, conv_dw_b,
           conv_ln_g, conv_ln_b, w_conv_o, b_conv_o, w_out, norm_ffn_g, w_ffn_in, w_ffn_out,
           final_norm_g):
    batch, seq, d = x.shape
    depth = w_in.shape[0]
    assert d == D_MODEL and seq % TM_MIX == 0 and seq % TM_FFN == 0
    assert seq % TQ_STEP == 0 and TQ_STEP % TQ_SUB == 0 and seq % GRID_W == 0

    cos, sin_a, sin_b = _rope_tables(seq)
    bd = _head_block_diag()
    mod = _ada_call(c, w_ada, b_ada)
    reps = LANES // HEAD_DIM
    gq, gk = jnp.tile(q_norm_g, (1, reps)), jnp.tile(k_norm_g, (1, reps))
    g_mix, g_ffn, g_final = norm_mix_g, norm_ffn_g, final_norm_g.reshape(1, -1)
    cb, lg, lb, bco = conv_dw_b, conv_ln_g, conv_ln_b, b_conv_o

    xt = x.reshape(batch * seq, d)
    for l in range(depth):
        q, kexp, vaug, u, gc, ga, w_ffn_in_b = _mix_in_call(l, xt, mod, g_mix, w_in, bd, gq, gk,
                                                            cos, sin_a, sin_b, w_ffn_in, seq)
        o, y = _attn_call(l, q, kexp, vaug, u, conv_dw, cb, batch, seq)
        xt = _merge_ffn_call(l, xt, mod, y, o, gc, ga, lg, lb, w_conv_o, bco, w_attn_o, w_out,
                             g_ffn, w_ffn_in_b, w_ffn_out, g_final, seq, final=(l == depth - 1))
    return xt.reshape(batch, seq, d)
```

```python
import functools
import math

import jax
import jax.numpy as jnp
from jax import lax
from jax.experimental import pallas as pl
from jax.experimental.pallas import tpu as pltpu

D_MODEL = 1024
GRID_W = 64
N_Q_HEADS = 8
N_KV_HEADS = 2
GROUP = N_Q_HEADS // N_KV_HEADS
HEAD_DIM = 64
ATTN_WIDTH = N_Q_HEADS * HEAD_DIM
KV_WIDTH = N_KV_HEADS * HEAD_DIM
CONV_WIDTH = D_MODEL // 2
CONV_KERNEL = 31
CONV_PAD = CONV_KERNEL // 2
ROPE_THETA = 10000.0
ROPE_AXIS_DIM = HEAD_DIM // 2
D_FF = 2816
IN_WIDTH = ATTN_WIDTH + 2 * KV_WIDTH + 2 * CONV_WIDTH + 2 * D_MODEL
EPS = 1e-6
N_MOD = 6

_OFF_K = ATTN_WIDTH
_OFF_V = _OFF_K + KV_WIDTH
_OFF_GLU_A = _OFF_V + KV_WIDTH
_OFF_GLU_B = _OFF_GLU_A + CONV_WIDTH
_OFF_GATE_CONV = _OFF_GLU_B + CONV_WIDTH
_OFF_GATE_ATTN = _OFF_GATE_CONV + D_MODEL

LANES = 128
SUBLANES = 8
HALO = 16
GROUP_WIDTH = GROUP * HEAD_DIM

TM_MIX = 512
TQ_STEP = 512
TQ_SUB = 256
KEY_TILE = 256
PV_CHUNK = 512
PV_ROW_SPLIT = 2
ANCHOR_LAG = 1
TM_FFN = 512
TN_ADA = 3072
CONV_ROWS = 64
FFN_CHUNK = 256
VMEM_LIMIT = 56 * 1024 * 1024

_Q_SCALE = HEAD_DIM ** -0.5 * math.log2(math.e)

_F32 = jnp.float32
_BF16 = jnp.bfloat16


def _resident(shape):
    nd = len(shape)
    return pl.BlockSpec(shape, lambda *_: (0,) * nd, pipeline_mode=pl.Buffered(1))


def _layer(shape, l):
    nd = len(shape)
    return pl.BlockSpec((1,) + shape, lambda *_: (l,) + (0,) * nd, pipeline_mode=pl.Buffered(1))


def _mod_spec(l, batch):
    return pl.BlockSpec((1, batch, N_MOD * D_MODEL), lambda *_: (l, 0, 0),
                        pipeline_mode=pl.Buffered(1))


def _mod_rows(mod_ref, tiles_per_seq, which):
    row = mod_ref[0, pl.ds(pl.program_id(0) // tiles_per_seq, 1), :]
    return [row[:, k * D_MODEL:(k + 1) * D_MODEL] for k in which]


def _params(n_axes, flags=None):
    return pltpu.CompilerParams(dimension_semantics=("arbitrary",) * n_axes,
                                vmem_limit_bytes=VMEM_LIMIT, flags=flags)


def _mm(a, w):
    return jnp.dot(a, w.astype(_BF16), preferred_element_type=_F32)


def _rms(x):
    return x * lax.rsqrt(jnp.mean(x * x, axis=-1, keepdims=True) + EPS)


def _split_bf16(a):
    hi = a.astype(_BF16)
    return hi, (a - hi.astype(_F32)).astype(_BF16)


def _ada_kernel(c_ref, w_ref, b_ref, o_ref):
    batch = c_ref.shape[0]
    c_hi, c_lo = _split_bf16(jax.nn.silu(c_ref[...]))
    w_hi, w_lo = _split_bf16(w_ref[0])
    both = jnp.dot(jnp.concatenate([c_hi, c_lo], axis=0), w_hi, preferred_element_type=_F32)
    o_ref[0] = (both[:batch] + both[batch:] + jnp.dot(c_hi, w_lo, preferred_element_type=_F32)
                + b_ref[pl.ds(pl.program_id(0), 1), :])


def _ada_call(c, w_ada, b_ada):
    depth, _, width = w_ada.shape
    batch = c.shape[0]
    return pl.pallas_call(
        _ada_kernel,
        grid=(depth, width // TN_ADA),
        in_specs=[
            pl.BlockSpec((batch, D_MODEL), lambda l, j: (0, 0)),
            pl.BlockSpec((1, D_MODEL, TN_ADA), lambda l, j: (l, 0, j)),
            pl.BlockSpec((depth, TN_ADA), lambda l, j: (0, j)),
        ],
        out_specs=pl.BlockSpec((1, batch, TN_ADA), lambda l, j: (l, 0, j)),
        out_shape=jax.ShapeDtypeStruct((depth, batch, width), _F32),
        compiler_params=_params(2),
        name="ada_mod",
    )(c, w_ada, b_ada)


def _head_norm_rope(z, bd, g, cos, sa, sb):
    sq = (z * z).astype(_BF16)
    wd = bd.shape[0]
    ms = jnp.concatenate([jnp.dot(sq[:, c0:c0 + wd], bd, preferred_element_type=_F32)
                          for c0 in range(0, z.shape[1], wd)], axis=1) * (1.0 / HEAD_DIM)
    zn = z * lax.rsqrt(ms + EPS)
    out = []
    for j in range(z.shape[1] // LANES):
        c = zn[:, j * LANES:(j + 1) * LANES] * g
        out.append(c * cos + pltpu.roll(c, LANES - ROPE_AXIS_DIM // 2, 1) * sa
                   + pltpu.roll(c, ROPE_AXIS_DIM // 2, 1) * sb)
    return out


def _mix_in_kernel(x_ref, mod_ref, g_ref, w_ref, bd_ref, gq_ref, gk_ref, cos_ref, sa_ref, sb_ref,
                   wf_ref, q_ref, kexp_ref, vaug_ref, u_ref, gc_ref, ga_ref, wfb_ref,
                   *, l, tiles_per_seq):
    wfb_ref[0] = wf_ref[0].astype(wfb_ref.dtype)

    shift, scale = _mod_rows(mod_ref, tiles_per_seq, (0, 1))
    h = (_rms(x_ref[...]) * g_ref[l:l + 1]) * (1.0 + scale) + shift
    hb = h.astype(_BF16)

    def proj(off, width):
        return _mm(hb, w_ref[0, :, off:off + width])

    zq, zkv = proj(0, ATTN_WIDTH), proj(_OFF_K, 2 * KV_WIDTH)
    zk, v = zkv[:, :KV_WIDTH], zkv[:, KV_WIDTH:]
    gc_ref[...] = jax.nn.sigmoid(proj(_OFF_GATE_CONV, D_MODEL)).astype(gc_ref.dtype)

    cos, sa, sb = cos_ref[...], sa_ref[...], sb_ref[...]
    q = _head_norm_rope(zq, bd_ref[...], gq_ref[l:l + 1], cos, sa, sb)
    for j, c in enumerate(q):
        q_ref[:, j * LANES:(j + 1) * LANES] = (c * _Q_SCALE).astype(q_ref.dtype)
    k, = _head_norm_rope(zk, bd_ref[0:KV_WIDTH, 0:KV_WIDTH], gk_ref[l:l + 1], cos, sa, sb)

    glu_gate = jax.nn.sigmoid(proj(_OFF_GLU_B, CONV_WIDTH))
    ga_ref[...] = jax.nn.sigmoid(proj(_OFF_GATE_ATTN, D_MODEL)).astype(ga_ref.dtype)
    u_ref[...] = proj(_OFF_GLU_A, CONV_WIDTH) * glu_gate

    low = lax.broadcasted_iota(jnp.int32, k.shape, 1) < HEAD_DIM
    k_sw, v_sw = pltpu.roll(k, HEAD_DIM, 1), pltpu.roll(v, HEAD_DIM, 1)
    for hk, (k2, v1) in enumerate(((jnp.where(low, k, k_sw), v), (jnp.where(low, k_sw, k), v_sw))):
        k2 = k2.astype(kexp_ref.dtype)
        for rep in range(GROUP_WIDTH // LANES):
            off = hk * GROUP_WIDTH + rep * LANES
            kexp_ref[:, off:off + LANES] = k2
        vaug_ref[:, hk * LANES:(hk + 1) * LANES] = jnp.where(low, v1, 1.0).astype(vaug_ref.dtype)


def _mix_in_call(l, x, mod, g, w_in, bd, gq, gk, cos, sa, sb, w_ffn_in, seq):
    tokens = x.shape[0]
    steps = tokens // TM_MIX
    tiles_per_seq = seq // TM_MIX
    wf_rows = D_MODEL // steps
    row = lambda i: (i, 0)
    pos = lambda i: (i % tiles_per_seq, 0)
    widths = (ATTN_WIDTH, N_KV_HEADS * GROUP_WIDTH, N_KV_HEADS * LANES, CONV_WIDTH, D_MODEL, D_MODEL)
    dtypes = (_BF16, _BF16, _BF16, _F32, _BF16, _BF16)
    return pl.pallas_call(
        functools.partial(_mix_in_kernel, l=l, tiles_per_seq=tiles_per_seq),
        grid=(steps,),
        in_specs=[
            pl.BlockSpec((TM_MIX, D_MODEL), row),
            _mod_spec(l, mod.shape[1]),
            _resident(g.shape),
            _layer((D_MODEL, IN_WIDTH), l),
            _resident((GROUP_WIDTH, GROUP_WIDTH)),
            _resident(gq.shape),
            _resident(gk.shape),
            pl.BlockSpec((TM_MIX, LANES), pos),
            pl.BlockSpec((TM_MIX, LANES), pos),
            pl.BlockSpec((TM_MIX, LANES), pos),
            pl.BlockSpec((1, wf_rows, 2 * D_FF), lambda i: (l, i, 0)),
        ],
        out_specs=[pl.BlockSpec((TM_MIX, w), row) for w in widths]
        + [pl.BlockSpec((1, wf_rows, 2 * D_FF), lambda i: (0, i, 0))],
        out_shape=[jax.ShapeDtypeStruct((tokens, w), dt) for w, dt in zip(widths, dtypes)]
        + [jax.ShapeDtypeStruct((1, D_MODEL, 2 * D_FF), _BF16)],
        compiler_params=_params(1),
        name="mix_in",
    )(x, mod, g, w_in, bd, gq, gk, cos, sa, sb, w_ffn_in)


def _conv_pieces(l, uc_ref, up_ref, un_ref, cw_ref, cb_ref, y_ref, win_ref, sh_ref, first, last):
    tm = uc_ref.shape[0]

    def window():
        win_ref[0:HALO] = jnp.where(first, 0.0, up_ref[...])
        win_ref[HALO:HALO + tm] = uc_ref[...]
        win_ref[HALO + tm:] = jnp.where(last, 0.0, un_ref[...])
        sh_rows = sh_ref.shape[1]
        for r in range(1, SUBLANES):
            sh_ref[r - 1] = win_ref[r:r + sh_rows, :]

    def block(c, rb):
        lanes = slice(c * LANES, (c + 1) * LANES)
        acc = jnp.broadcast_to(cb_ref[l:l + 1, lanes], (CONV_ROWS, LANES))
        for k in range(CONV_KERNEL):
            a, r = divmod(HALO - CONV_PAD + k, SUBLANES)
            rows = slice(rb + a * SUBLANES, rb + a * SUBLANES + CONV_ROWS)
            tap = win_ref[rows, lanes] if r == 0 else sh_ref[r - 1, rows, lanes]
            acc = acc + tap * cw_ref[0, k:k + 1, lanes]
        y_ref[rb:rb + CONV_ROWS, lanes] = acc
        word = lax.bitcast_convert_type(acc, jnp.int32)
        bits = functools.reduce(jnp.bitwise_or, [word[i:i + SUBLANES]
                                                 for i in range(0, CONV_ROWS, SUBLANES)])
        return lax.shift_right_logical(lax.shift_right_logical(bits, 16), 16)

    blocks = [functools.partial(block, c, rb)
              for c in range(CONV_WIDTH // LANES) for rb in range(0, tm, CONV_ROWS)]
    return [window] + blocks


def _spread(items, n):
    bounds = [round(i * len(items) / n) for i in range(n + 1)]
    return [items[bounds[i]:bounds[i + 1]] for i in range(n)]


def _attn_kernel(q_ref, kexp_ref, vaug_ref, uc_ref, up_ref, un_ref, cw_ref, cb_ref,
                 o_ref, y_ref, s_ref, p_ref, acc_ref, win_ref, sh_ref, *, l, conv_tiles_per_seq):
    seq = kexp_ref.shape[0]
    rows = GROUP * TQ_SUB
    assert TQ_STEP == 2 * TQ_SUB
    t = pl.program_id(1) * pl.num_programs(2) + pl.program_id(2)
    conv = _conv_pieces(l, uc_ref, up_ref, un_ref, cw_ref, cb_ref, y_ref, win_ref, sh_ref,
                        t == 0, t == conv_tiles_per_seq - 1)
    lane_head = lax.broadcasted_iota(jnp.int32, (TQ_SUB, GROUP_WIDTH), 1) // HEAD_DIM
    low = lax.broadcasted_iota(jnp.int32, (TQ_SUB, LANES), 1) < HEAD_DIM

    def stacked(sub):
        q = q_ref[sub * TQ_SUB:(sub + 1) * TQ_SUB, :]
        return jnp.concatenate(
            [jnp.where(lane_head == g, q, jnp.zeros_like(q)) for g in range(GROUP)], axis=0)

    def score_tile(sub, stack, j, after=None):
        keys = slice(j * KEY_TILE, (j + 1) * KEY_TILE)
        k_tile = kexp_ref[keys, :]
        if after is not None:
            zero = jnp.tile(after.astype(_F32), (2, 1)).astype(_BF16)
            k_tile = k_tile + jnp.tile(zero, (KEY_TILE // (2 * SUBLANES), GROUP_WIDTH // LANES))
        s_ref[sub, :, keys] = lax.dot_general(stack, k_tile, (((1,), (1,)), ((), ())),
                                              preferred_element_type=_F32)

    def softmax_rows(sub, j, n):
        r = slice(j * rows // n, (j + 1) * rows // n)
        s = s_ref[sub, r, :]
        p_ref[sub, r, :] = jnp.exp2(s - jnp.max(s, axis=-1, keepdims=True)).astype(_BF16)

    def pv_chunk(sub, half, j):
        r = slice(half * rows // PV_ROW_SPLIT, (half + 1) * rows // PV_ROW_SPLIT)
        keys = slice(j * PV_CHUNK, (j + 1) * PV_CHUNK)
        part = jnp.dot(p_ref[sub, r, keys], vaug_ref[keys, :], preferred_element_type=_F32)
        if j == 0:
            acc_ref[sub, r, :] = part
        else:
            acc_ref[sub, r, :] += part

    def write_out(sub):
        acc = acc_ref[sub]
        r = acc / pltpu.roll(acc, HEAD_DIM, 1)
        head = [r[g * TQ_SUB:(g + 1) * TQ_SUB] for g in range(GROUP)]
        out_rows = slice(sub * TQ_SUB, (sub + 1) * TQ_SUB)
        for pair in range(GROUP // 2):
            both = jnp.where(low, head[2 * pair], pltpu.roll(head[2 * pair + 1], HEAD_DIM, 1))
            o_ref[out_rows, pair * LANES:(pair + 1) * LANES] = both.astype(o_ref.dtype)

    n_score = seq // KEY_TILE
    n_pv = seq // PV_CHUNK

    stacks = [stacked(0), stacked(1)]
    tokens = []
    for j, pieces in enumerate(_spread(conv, 2 * n_score)):
        after = tokens[j - 1 - ANCHOR_LAG] if j > ANCHOR_LAG else None
        score_tile(j // n_score, stacks[j // n_score], j % n_score, after=after)
        done = [t for t in (piece() for piece in pieces) if t is not None]
        tokens.append(functools.reduce(jnp.bitwise_or, done) if done else None)
    for sub in range(2):
        for half, chunks in enumerate(_spread(list(range(n_score)), PV_ROW_SPLIT)):
            for i in chunks:
                softmax_rows(sub, i, n_score)
            for j in range(n_pv):
                pv_chunk(sub, half, j)
        write_out(sub)


def _conv_tile_specs(tm, tile, n_halo):
    halo_per_tile = tm // HALO
    return [
        pl.BlockSpec((tm, CONV_WIDTH), lambda *ids: (tile(*ids), 0)),
        pl.BlockSpec((HALO, CONV_WIDTH),
                     lambda *ids: (jnp.maximum(tile(*ids) * halo_per_tile - 1, 0), 0)),
        pl.BlockSpec((HALO, CONV_WIDTH),
                     lambda *ids: (jnp.minimum((tile(*ids) + 1) * halo_per_tile, n_halo - 1), 0)),
    ]


def _attn_call(l, q, kexp, vaug, u, cw, cb, batch, seq):
    tokens = q.shape[0]
    nq = seq // TQ_STEP
    conv_tiles_per_seq = N_KV_HEADS * nq
    tm = seq // conv_tiles_per_seq
    conv_tile = lambda b, h, i: b * conv_tiles_per_seq + h * nq + i
    sh_rows = tm + ((CONV_KERNEL + HALO - CONV_PAD - 1) // SUBLANES) * SUBLANES
    return pl.pallas_call(
        functools.partial(_attn_kernel, l=l, conv_tiles_per_seq=conv_tiles_per_seq),
        grid=(batch, N_KV_HEADS, nq),
        in_specs=[
            pl.BlockSpec((TQ_STEP, GROUP_WIDTH), lambda b, h, i: (b * nq + i, h)),
            pl.BlockSpec((seq, GROUP_WIDTH), lambda b, h, i: (b, h)),
            pl.BlockSpec((seq, LANES), lambda b, h, i: (b, h)),
            *_conv_tile_specs(tm, conv_tile, tokens // HALO),
            _layer((CONV_KERNEL, CONV_WIDTH), l),
            _resident(cb.shape),
        ],
        out_specs=[
            pl.BlockSpec((TQ_STEP, GROUP_WIDTH), lambda b, h, i: (b * nq + i, h)),
            pl.BlockSpec((tm, CONV_WIDTH), lambda b, h, i: (conv_tile(b, h, i), 0)),
        ],
        out_shape=[
            jax.ShapeDtypeStruct((tokens, ATTN_WIDTH), _BF16),
            jax.ShapeDtypeStruct((tokens, CONV_WIDTH), _F32),
        ],
        scratch_shapes=[
            pltpu.VMEM((2, GROUP * TQ_SUB, seq), _F32),
            pltpu.VMEM((2, GROUP * TQ_SUB, seq), _BF16),
            pltpu.VMEM((2, GROUP * TQ_SUB, LANES), _F32),
            pltpu.VMEM((tm + 2 * HALO, CONV_WIDTH), _F32),
            pltpu.VMEM((SUBLANES - 1, sh_rows, CONV_WIDTH), _F32),
        ],
        compiler_params=_params(3),
        name="attention",
    )(q, kexp, vaug, u, u, u, cw, cb)


def _merge_ffn_kernel(x_ref, mod_ref, y_ref, o_ref, gc_ref, ga_ref, lg_ref, lb_ref, wco_ref, bco_ref,
                      wao_ref, wout_ref, g_ref, w1_ref, w2_ref, fg_ref, out_ref, h2_ref,
                      *, l, tiles_per_seq, final):
    gate_m, shift, scale, gate_f = _mod_rows(mod_ref, tiles_per_seq, (2, 3, 4, 5))

    attn_out = _mm(o_ref[...], wao_ref[0])
    y = y_ref[...]
    yc = y - jnp.mean(y, axis=-1, keepdims=True)
    ln = (yc * lax.rsqrt(jnp.mean(yc * yc, axis=-1, keepdims=True) + EPS) * lg_ref[l:l + 1]
          + lb_ref[l:l + 1])
    conv_out = _mm(jax.nn.silu(ln).astype(_BF16), wco_ref[0]) + bco_ref[l:l + 1]
    merged = gc_ref[...].astype(_F32) * conv_out + ga_ref[...].astype(_F32) * attn_out
    x = x_ref[...] + gate_m * _mm(merged.astype(_BF16), wout_ref[0])

    hb = ((_rms(x) * g_ref[l:l + 1]) * (1.0 + scale) + shift).astype(_BF16)
    for c0 in range(0, D_FF, FFN_CHUNK):
        cw = min(FFN_CHUNK, D_FF - c0)
        gg = _mm(hb, w1_ref[0, :, c0:c0 + cw])
        uu = _mm(hb, w1_ref[0, :, D_FF + c0:D_FF + c0 + cw])
        h2_ref[:, c0:c0 + cw] = (jax.nn.silu(gg) * uu).astype(_BF16)
    y = x + gate_f * _mm(h2_ref[...], w2_ref[0])
    if final:
        y = _rms(y) * fg_ref[...]
    out_ref[...] = y


def _merge_ffn_call(l, x, mod, y, o, gc, ga, lg, lb, wco, bco, wao, wout, g, w1, w2, fg, seq,
                    final):
    tokens = x.shape[0]
    tm = TM_FFN
    row = lambda i: (i, 0)
    return pl.pallas_call(
        functools.partial(_merge_ffn_kernel, l=l, tiles_per_seq=seq // tm, final=final),
        grid=(tokens // tm,),
        in_specs=[
            pl.BlockSpec((tm, D_MODEL), row),
            _mod_spec(l, mod.shape[1]),
            pl.BlockSpec((tm, CONV_WIDTH), row),
            pl.BlockSpec((tm, ATTN_WIDTH), row),
            pl.BlockSpec((tm, D_MODEL), row),
            pl.BlockSpec((tm, D_MODEL), row),
            _resident(lg.shape),
            _resident(lb.shape),
            _layer((CONV_WIDTH, D_MODEL), l),
            _resident(bco.shape),
            _layer((ATTN_WIDTH, D_MODEL), l),
            _layer((D_MODEL, D_MODEL), l),
            _resident(g.shape),
            _layer((D_MODEL, 2 * D_FF), 0),
            _layer((D_FF, D_MODEL), l),
            _resident((1, D_MODEL)),
        ],
        out_specs=pl.BlockSpec((tm, D_MODEL), row),
        out_shape=jax.ShapeDtypeStruct((tokens, D_MODEL), _F32),
        scratch_shapes=[pltpu.VMEM((tm, D_FF), _BF16)],
        compiler_params=_params(1),
        name="merge_ffn_final" if final else "merge_ffn",
    )(x, mod, y, o, gc, ga, lg, lb, wco, bco, wao, wout, g, w1, w2, fg)


def _rope_tables(seq):
    rows = seq // GRID_W
    row_pos = jnp.broadcast_to(jnp.arange(rows)[:, None], (rows, GRID_W)).reshape(-1).astype(_F32)
    col_pos = jnp.broadcast_to(jnp.arange(GRID_W)[None, :], (rows, GRID_W)).reshape(-1).astype(_F32)
    inv_freq = ROPE_THETA ** (-jnp.arange(0, ROPE_AXIS_DIM, 2, dtype=_F32) / ROPE_AXIS_DIM)
    ang_r = row_pos[:, None] * inv_freq[None, :]
    ang_c = col_pos[:, None] * inv_freq[None, :]
    cos_r, sin_r, cos_c, sin_c = jnp.cos(ang_r), jnp.sin(ang_r), jnp.cos(ang_c), jnp.sin(ang_c)
    zero = jnp.zeros_like(sin_r)
    cos = jnp.concatenate([cos_r, cos_r, cos_c, cos_c], axis=-1)
    sin_a = jnp.concatenate([-sin_r, zero, -sin_c, zero], axis=-1)
    sin_b = jnp.concatenate([zero, sin_r, zero, sin_c], axis=-1)
    reps = LANES // HEAD_DIM
    return jnp.tile(cos, (1, reps)), jnp.tile(sin_a, (1, reps)), jnp.tile(sin_b, (1, reps))


def _head_block_diag():
    head = jnp.arange(GROUP_WIDTH) // HEAD_DIM
    return (head[:, None] == head[None, :]).astype(_BF16)


def kernel(x, c, w_ada, b_ada, norm_mix_g, w_in, q_norm_g, k_norm_g, w_attn_o, conv_dw, conv_dw_b,
           conv_ln_g, conv_ln_b, w_conv_o, b_conv_o, w_out, norm_ffn_g, w_ffn_in, w_ffn_out,
           final_norm_g):
    batch, seq, d = x.shape
    depth = w_in.shape[0]
    assert d == D_MODEL and seq % TM_MIX == 0 and seq % TM_FFN == 0
    assert seq % TQ_STEP == 0 and TQ_STEP % TQ_SUB == 0 and seq % GRID_W == 0

    cos, sin_a, sin_b = _rope_tables(seq)
    bd = _head_block_diag()
    mod = _ada_call(c, w_ada, b_ada)
    reps = LANES // HEAD_DIM
    gq, gk = jnp.tile(q_norm_g, (1, reps)), jnp.tile(k_norm_g, (1, reps))
    g_mix, g_ffn, g_final = norm_mix_g, norm_ffn_g, final_norm_g.reshape(1, -1)
    cb, lg, lb, bco = conv_dw_b, conv_ln_g, conv_ln_b, b_conv_o

    xt = x.reshape(batch * seq, d)
    for l in range(depth):
        q, kexp, vaug, u, gc, ga, w_ffn_in_b = _mix_in_call(l, xt, mod, g_mix, w_in, bd, gq, gk,
                                                            cos, sin_a, sin_b, w_ffn_in, seq)
        o, y = _attn_call(l, q, kexp, vaug, u, conv_dw, cb, batch, seq)
        xt = _merge_ffn_call(l, xt, mod, y, o, gc, ga, lg, lb, w_conv_o, bco, w_attn_o, w_out,
                             g_ffn, w_ffn_in_b, w_ffn_out, g_final, seq, final=(l == depth - 1))
    return xt.reshape(batch, seq, d)
```

```python
import functools
import math

import jax
import jax.numpy as jnp
from jax import lax
from jax.experimental import pallas as pl
from jax.experimental.pallas import tpu as pltpu

D_MODEL = 1024
GRID_W = 64
N_Q_HEADS = 8
N_KV_HEADS = 2
GROUP = N_Q_HEADS // N_KV_HEADS
HEAD_DIM = 64
ATTN_WIDTH = N_Q_HEADS * HEAD_DIM
KV_WIDTH = N_KV_HEADS * HEAD_DIM
CONV_WIDTH = D_MODEL // 2
CONV_KERNEL = 31
CONV_PAD = CONV_KERNEL // 2
ROPE_THETA = 10000.0
ROPE_AXIS_DIM = HEAD_DIM // 2
D_FF = 2816
IN_WIDTH = ATTN_WIDTH + 2 * KV_WIDTH + 2 * CONV_WIDTH + 2 * D_MODEL
EPS = 1e-6
N_MOD = 6

_OFF_K = ATTN_WIDTH
_OFF_V = _OFF_K + KV_WIDTH
_OFF_GLU_A = _OFF_V + KV_WIDTH
_OFF_GLU_B = _OFF_GLU_A + CONV_WIDTH
_OFF_GATE_CONV = _OFF_GLU_B + CONV_WIDTH
_OFF_GATE_ATTN = _OFF_GATE_CONV + D_MODEL

LANES = 128
SUBLANES = 8
HALO = 16
GROUP_WIDTH = GROUP * HEAD_DIM

TM_MIX = 512
TQ_STEP = 512
TQ_SUB = 256
KEY_TILE = 256
PV_CHUNK = 512
PV_ROW_SPLIT = 2
ANCHOR_LAG = 1
TM_FFN = 512
TN_ADA = 3072
CONV_ROWS = 32
FFN_CHUNK = 256
VMEM_LIMIT = 56 * 1024 * 1024

_Q_SCALE = HEAD_DIM ** -0.5 * math.log2(math.e)

_F32 = jnp.float32
_BF16 = jnp.bfloat16


def _resident(shape):
    nd = len(shape)
    return pl.BlockSpec(shape, lambda *_: (0,) * nd, pipeline_mode=pl.Buffered(1))


def _layer(shape, l):
    nd = len(shape)
    return pl.BlockSpec((1,) + shape, lambda *_: (l,) + (0,) * nd, pipeline_mode=pl.Buffered(1))


def _mod_spec(l, batch):
    return pl.BlockSpec((1, batch, N_MOD * D_MODEL), lambda *_: (l, 0, 0),
                        pipeline_mode=pl.Buffered(1))


def _mod_rows(mod_ref, tiles_per_seq, which):
    row = mod_ref[0, pl.ds(pl.program_id(0) // tiles_per_seq, 1), :]
    return [row[:, k * D_MODEL:(k + 1) * D_MODEL] for k in which]


def _params(n_axes, flags=None):
    return pltpu.CompilerParams(dimension_semantics=("arbitrary",) * n_axes,
                                vmem_limit_bytes=VMEM_LIMIT, flags=flags)


def _mm(a, w):
    return jnp.dot(a, w.astype(_BF16), preferred_element_type=_F32)


def _rms(x):
    return x * lax.rsqrt(jnp.mean(x * x, axis=-1, keepdims=True) + EPS)


def _split_bf16(a):
    hi = a.astype(_BF16)
    return hi, (a - hi.astype(_F32)).astype(_BF16)


def _ada_kernel(c_ref, w_ref, b_ref, o_ref):
    batch = c_ref.shape[0]
    c_hi, c_lo = _split_bf16(jax.nn.silu(c_ref[...]))
    w_hi, w_lo = _split_bf16(w_ref[0])
    both = jnp.dot(jnp.concatenate([c_hi, c_lo], axis=0), w_hi, preferred_element_type=_F32)
    o_ref[0] = (both[:batch] + both[batch:] + jnp.dot(c_hi, w_lo, preferred_element_type=_F32)
                + b_ref[pl.ds(pl.program_id(0), 1), :])


def _ada_call(c, w_ada, b_ada):
    depth, _, width = w_ada.shape
    batch = c.shape[0]
    return pl.pallas_call(
        _ada_kernel,
        grid=(depth, width // TN_ADA),
        in_specs=[
            pl.BlockSpec((batch, D_MODEL), lambda l, j: (0, 0)),
            pl.BlockSpec((1, D_MODEL, TN_ADA), lambda l, j: (l, 0, j)),
            pl.BlockSpec((depth, TN_ADA), lambda l, j: (0, j)),
        ],
        out_specs=pl.BlockSpec((1, batch, TN_ADA), lambda l, j: (l, 0, j)),
        out_shape=jax.ShapeDtypeStruct((depth, batch, width), _F32),
        compiler_params=_params(2),
        name="ada_mod",
    )(c, w_ada, b_ada)


def _head_norm_rope(z, bd, g, cos, sa, sb):
    sq = (z * z).astype(_BF16)
    wd = bd.shape[0]
    ms = jnp.concatenate([jnp.dot(sq[:, c0:c0 + wd], bd, preferred_element_type=_F32)
                          for c0 in range(0, z.shape[1], wd)], axis=1) * (1.0 / HEAD_DIM)
    zn = z * lax.rsqrt(ms + EPS)
    out = []
    for j in range(z.shape[1] // LANES):
        c = zn[:, j * LANES:(j + 1) * LANES] * g
        out.append(c * cos + pltpu.roll(c, LANES - ROPE_AXIS_DIM // 2, 1) * sa
                   + pltpu.roll(c, ROPE_AXIS_DIM // 2, 1) * sb)
    return out


def _mix_in_kernel(x_ref, mod_ref, g_ref, w_ref, bd_ref, gq_ref, gk_ref, cos_ref, sa_ref, sb_ref,
                   wf_ref, q_ref, kexp_ref, vaug_ref, u_ref, gc_ref, ga_ref, wfb_ref,
                   *, l, tiles_per_seq):
    wfb_ref[0] = wf_ref[0].astype(wfb_ref.dtype)

    shift, scale = _mod_rows(mod_ref, tiles_per_seq, (0, 1))
    h = (_rms(x_ref[...]) * g_ref[l:l + 1]) * (1.0 + scale) + shift
    hb = h.astype(_BF16)

    def proj(off, width):
        return _mm(hb, w_ref[0, :, off:off + width])

    zq, zkv = proj(0, ATTN_WIDTH), proj(_OFF_K, 2 * KV_WIDTH)
    zk, v = zkv[:, :KV_WIDTH], zkv[:, KV_WIDTH:]
    gc_ref[...] = jax.nn.sigmoid(proj(_OFF_GATE_CONV, D_MODEL)).astype(gc_ref.dtype)

    cos, sa, sb = cos_ref[...], sa_ref[...], sb_ref[...]
    q = _head_norm_rope(zq, bd_ref[...], gq_ref[l:l + 1], cos, sa, sb)
    for j, c in enumerate(q):
        q_ref[:, j * LANES:(j + 1) * LANES] = (c * _Q_SCALE).astype(q_ref.dtype)
    k, = _head_norm_rope(zk, bd_ref[0:KV_WIDTH, 0:KV_WIDTH], gk_ref[l:l + 1], cos, sa, sb)

    glu_gate = jax.nn.sigmoid(proj(_OFF_GLU_B, CONV_WIDTH))
    ga_ref[...] = jax.nn.sigmoid(proj(_OFF_GATE_ATTN, D_MODEL)).astype(ga_ref.dtype)
    u_ref[...] = proj(_OFF_GLU_A, CONV_WIDTH) * glu_gate

    low = lax.broadcasted_iota(jnp.int32, k.shape, 1) < HEAD_DIM
    k_sw, v_sw = pltpu.roll(k, HEAD_DIM, 1), pltpu.roll(v, HEAD_DIM, 1)
    for hk, (k2, v1) in enumerate(((jnp.where(low, k, k_sw), v), (jnp.where(low, k_sw, k), v_sw))):
        k2 = k2.astype(kexp_ref.dtype)
        for rep in range(GROUP_WIDTH // LANES):
            off = hk * GROUP_WIDTH + rep * LANES
            kexp_ref[:, off:off + LANES] = k2
        vaug_ref[:, hk * LANES:(hk + 1) * LANES] = jnp.where(low, v1, 1.0).astype(vaug_ref.dtype)


def _mix_in_call(l, x, mod, g, w_in, bd, gq, gk, cos, sa, sb, w_ffn_in, seq):
    tokens = x.shape[0]
    steps = tokens // TM_MIX
    tiles_per_seq = seq // TM_MIX
    wf_rows = D_MODEL // steps
    row = lambda i: (i, 0)
    pos = lambda i: (i % tiles_per_seq, 0)
    widths = (ATTN_WIDTH, N_KV_HEADS * GROUP_WIDTH, N_KV_HEADS * LANES, CONV_WIDTH, D_MODEL, D_MODEL)
    dtypes = (_BF16, _BF16, _BF16, _F32, _BF16, _BF16)
    return pl.pallas_call(
        functools.partial(_mix_in_kernel, l=l, tiles_per_seq=tiles_per_seq),
        grid=(steps,),
        in_specs=[
            pl.BlockSpec((TM_MIX, D_MODEL), row),
            _mod_spec(l, mod.shape[1]),
            _resident(g.shape),
            _layer((D_MODEL, IN_WIDTH), l),
            _resident((GROUP_WIDTH, GROUP_WIDTH)),
            _resident(gq.shape),
            _resident(gk.shape),
            pl.BlockSpec((TM_MIX, LANES), pos),
            pl.BlockSpec((TM_MIX, LANES), pos),
            pl.BlockSpec((TM_MIX, LANES), pos),
            pl.BlockSpec((1, wf_rows, 2 * D_FF), lambda i: (l, i, 0)),
        ],
        out_specs=[pl.BlockSpec((TM_MIX, w), row) for w in widths]
        + [pl.BlockSpec((1, wf_rows, 2 * D_FF), lambda i: (0, i, 0))],
        out_shape=[jax.ShapeDtypeStruct((tokens, w), dt) for w, dt in zip(widths, dtypes)]
        + [jax.ShapeDtypeStruct((1, D_MODEL, 2 * D_FF), _BF16)],
        compiler_params=_params(1),
        name="mix_in",
    )(x, mod, g, w_in, bd, gq, gk, cos, sa, sb, w_ffn_in)


def _conv_pieces(l, uc_ref, up_ref, un_ref, cw_ref, cb_ref, y_ref, win_ref, sh_ref, first, last):
    tm = uc_ref.shape[0]

    def window():
        win_ref[0:HALO] = jnp.where(first, 0.0, up_ref[...])
        win_ref[HALO:HALO + tm] = uc_ref[...]
        win_ref[HALO + tm:] = jnp.where(last, 0.0, un_ref[...])
        sh_rows = sh_ref.shape[1]
        for r in range(1, SUBLANES):
            sh_ref[r - 1] = win_ref[r:r + sh_rows, :]

    def block(c, rb):
        lanes = slice(c * LANES, (c + 1) * LANES)
        acc = jnp.broadcast_to(cb_ref[l:l + 1, lanes], (CONV_ROWS, LANES))
        for k in range(CONV_KERNEL):
            a, r = divmod(HALO - CONV_PAD + k, SUBLANES)
            rows = slice(rb + a * SUBLANES, rb + a * SUBLANES + CONV_ROWS)
            tap = win_ref[rows, lanes] if r == 0 else sh_ref[r - 1, rows, lanes]
            acc = acc + tap * cw_ref[0, k:k + 1, lanes]
        y_ref[rb:rb + CONV_ROWS, lanes] = acc
        word = lax.bitcast_convert_type(acc, jnp.int32)
        bits = functools.reduce(jnp.bitwise_or, [word[i:i + SUBLANES]
                                                 for i in range(0, CONV_ROWS, SUBLANES)])
        return lax.shift_right_logical(lax.shift_right_logical(bits, 16), 16)

    blocks = [functools.partial(block, c, rb)
              for c in range(CONV_WIDTH // LANES) for rb in range(0, tm, CONV_ROWS)]
    return [window] + blocks


def _spread(items, n):
    bounds = [round(i * len(items) / n) for i in range(n + 1)]
    return [items[bounds[i]:bounds[i + 1]] for i in range(n)]


def _attn_kernel(q_ref, kexp_ref, vaug_ref, uc_ref, up_ref, un_ref, cw_ref, cb_ref,
                 o_ref, y_ref, s_ref, p_ref, acc_ref, win_ref, sh_ref, *, l, conv_tiles_per_seq):
    seq = kexp_ref.shape[0]
    rows = GROUP * TQ_SUB
    assert TQ_STEP == 2 * TQ_SUB
    t = pl.program_id(1) * pl.num_programs(2) + pl.program_id(2)
    conv = _conv_pieces(l, uc_ref, up_ref, un_ref, cw_ref, cb_ref, y_ref, win_ref, sh_ref,
                        t == 0, t == conv_tiles_per_seq - 1)
    lane_head = lax.broadcasted_iota(jnp.int32, (TQ_SUB, GROUP_WIDTH), 1) // HEAD_DIM
    low = lax.broadcasted_iota(jnp.int32, (TQ_SUB, LANES), 1) < HEAD_DIM

    def stacked(sub):
        q = q_ref[sub * TQ_SUB:(sub + 1) * TQ_SUB, :]
        return jnp.concatenate(
            [jnp.where(lane_head == g, q, jnp.zeros_like(q)) for g in range(GROUP)], axis=0)

    def score_tile(sub, stack, j, after=None):
        keys = slice(j * KEY_TILE, (j + 1) * KEY_TILE)
        k_tile = kexp_ref[keys, :]
        if after is not None:
            zero = jnp.tile(after.astype(_F32), (2, 1)).astype(_BF16)
            k_tile = k_tile + jnp.tile(zero, (KEY_TILE // (2 * SUBLANES), GROUP_WIDTH // LANES))
        s_ref[sub, :, keys] = lax.dot_general(stack, k_tile, (((1,), (1,)), ((), ())),
                                              preferred_element_type=_F32)

    def softmax_rows(sub, j, n):
        r = slice(j * rows // n, (j + 1) * rows // n)
        s = s_ref[sub, r, :]
        p_ref[sub, r, :] = jnp.exp2(s - jnp.max(s, axis=-1, keepdims=True)).astype(_BF16)

    def pv_chunk(sub, half, j):
        r = slice(half * rows // PV_ROW_SPLIT, (half + 1) * rows // PV_ROW_SPLIT)
        keys = slice(j * PV_CHUNK, (j + 1) * PV_CHUNK)
        part = jnp.dot(p_ref[sub, r, keys], vaug_ref[keys, :], preferred_element_type=_F32)
        if j == 0:
            acc_ref[sub, r, :] = part
        else:
            acc_ref[sub, r, :] += part

    def write_out(sub):
        acc = acc_ref[sub]
        r = acc / pltpu.roll(acc, HEAD_DIM, 1)
        head = [r[g * TQ_SUB:(g + 1) * TQ_SUB] for g in range(GROUP)]
        out_rows = slice(sub * TQ_SUB, (sub + 1) * TQ_SUB)
        for pair in range(GROUP // 2):
            both = jnp.where(low, head[2 * pair], pltpu.roll(head[2 * pair + 1], HEAD_DIM, 1))
            o_ref[out_rows, pair * LANES:(pair + 1) * LANES] = both.astype(o_ref.dtype)

    n_score = seq // KEY_TILE
    n_pv = seq // PV_CHUNK

    stacks = [stacked(0), stacked(1)]
    tokens = []
    for j, pieces in enumerate(_spread(conv, 2 * n_score)):
        after = tokens[j - 1 - ANCHOR_LAG] if j > ANCHOR_LAG else None
        score_tile(j // n_score, stacks[j // n_score], j % n_score, after=after)
        done = [t for t in (piece() for piece in pieces) if t is not None]
        tokens.append(functools.reduce(jnp.bitwise_or, done) if done else None)
    for sub in range(2):
        for half, chunks in enumerate(_spread(list(range(n_score)), PV_ROW_SPLIT)):
            for i in chunks:
                softmax_rows(sub, i, n_score)
            for j in range(n_pv):
                pv_chunk(sub, half, j)
        write_out(sub)


def _conv_tile_specs(tm, tile, n_halo):
    halo_per_tile = tm // HALO
    return [
        pl.BlockSpec((tm, CONV_WIDTH), lambda *ids: (tile(*ids), 0)),
        pl.BlockSpec((HALO, CONV_WIDTH),
                     lambda *ids: (jnp.maximum(tile(*ids) * halo_per_tile - 1, 0), 0)),
        pl.BlockSpec((HALO, CONV_WIDTH),
                     lambda *ids: (jnp.minimum((tile(*ids) + 1) * halo_per_tile, n_halo - 1), 0)),
    ]


def _attn_call(l, q, kexp, vaug, u, cw, cb, batch, seq):
    tokens = q.shape[0]
    nq = seq // TQ_STEP
    conv_tiles_per_seq = N_KV_HEADS * nq
    tm = seq // conv_tiles_per_seq
    conv_tile = lambda b, h, i: b * conv_tiles_per_seq + h * nq + i
    sh_rows = tm + ((CONV_KERNEL + HALO - CONV_PAD - 1) // SUBLANES) * SUBLANES
    return pl.pallas_call(
        functools.partial(_attn_kernel, l=l, conv_tiles_per_seq=conv_tiles_per_seq),
        grid=(batch, N_KV_HEADS, nq),
        in_specs=[
            pl.BlockSpec((TQ_STEP, GROUP_WIDTH), lambda b, h, i: (b * nq + i, h)),
            pl.BlockSpec((seq, GROUP_WIDTH), lambda b, h, i: (b, h)),
            pl.BlockSpec((seq, LANES), lambda b, h, i: (b, h)),
            *_conv_tile_specs(tm, conv_tile, tokens // HALO),
            _layer((CONV_KERNEL, CONV_WIDTH), l),
            _resident(cb.shape),
        ],
        out_specs=[
            pl.BlockSpec((TQ_STEP, GROUP_WIDTH), lambda b, h, i: (b * nq + i, h)),
            pl.BlockSpec((tm, CONV_WIDTH), lambda b, h, i: (conv_tile(b, h, i), 0)),
        ],
        out_shape=[
            jax.ShapeDtypeStruct((tokens, ATTN_WIDTH), _BF16),
            jax.ShapeDtypeStruct((tokens, CONV_WIDTH), _F32),
        ],
        scratch_shapes=[
            pltpu.VMEM((2, GROUP * TQ_SUB, seq), _F32),
            pltpu.VMEM((2, GROUP * TQ_SUB, seq), _BF16),
            pltpu.VMEM((2, GROUP * TQ_SUB, LANES), _F32),
            pltpu.VMEM((tm + 2 * HALO, CONV_WIDTH), _F32),
            pltpu.VMEM((SUBLANES - 1, sh_rows, CONV_WIDTH), _F32),
        ],
        compiler_params=_params(3),
        name="attention",
    )(q, kexp, vaug, u, u, u, cw, cb)


def _merge_ffn_kernel(x_ref, mod_ref, y_ref, o_ref, gc_ref, ga_ref, lg_ref, lb_ref, wco_ref, bco_ref,
                      wao_ref, wout_ref, g_ref, w1_ref, w2_ref, fg_ref, out_ref, h2_ref,
                      *, l, tiles_per_seq, final):
    gate_m, shift, scale, gate_f = _mod_rows(mod_ref, tiles_per_seq, (2, 3, 4, 5))

    attn_out = _mm(o_ref[...], wao_ref[0])
    y = y_ref[...]
    yc = y - jnp.mean(y, axis=-1, keepdims=True)
    ln = (yc * lax.rsqrt(jnp.mean(yc * yc, axis=-1, keepdims=True) + EPS) * lg_ref[l:l + 1]
          + lb_ref[l:l + 1])
    conv_out = _mm(jax.nn.silu(ln).astype(_BF16), wco_ref[0]) + bco_ref[l:l + 1]
    merged = gc_ref[...].astype(_F32) * conv_out + ga_ref[...].astype(_F32) * attn_out
    x = x_ref[...] + gate_m * _mm(merged.astype(_BF16), wout_ref[0])

    hb = ((_rms(x) * g_ref[l:l + 1]) * (1.0 + scale) + shift).astype(_BF16)
    for c0 in range(0, D_FF, FFN_CHUNK):
        cw = min(FFN_CHUNK, D_FF - c0)
        gg = _mm(hb, w1_ref[0, :, c0:c0 + cw])
        uu = _mm(hb, w1_ref[0, :, D_FF + c0:D_FF + c0 + cw])
        h2_ref[:, c0:c0 + cw] = (jax.nn.silu(gg) * uu).astype(_BF16)
    y = x + gate_f * _mm(h2_ref[...], w2_ref[0])
    if final:
        y = _rms(y) * fg_ref[...]
    out_ref[...] = y


def _merge_ffn_call(l, x, mod, y, o, gc, ga, lg, lb, wco, bco, wao, wout, g, w1, w2, fg, seq,
                    final):
    tokens = x.shape[0]
    tm = TM_FFN
    row = lambda i: (i, 0)
    return pl.pallas_call(
        functools.partial(_merge_ffn_kernel, l=l, tiles_per_seq=seq // tm, final=final),
        grid=(tokens // tm,),
        in_specs=[
            pl.BlockSpec((tm, D_MODEL), row),
            _mod_spec(l, mod.shape[1]),
            pl.BlockSpec((tm, CONV_WIDTH), row),
            pl.BlockSpec((tm, ATTN_WIDTH), row),
            pl.BlockSpec((tm, D_MODEL), row),
            pl.BlockSpec((tm, D_MODEL), row),
            _resident(lg.shape),
            _resident(lb.shape),
            _layer((CONV_WIDTH, D_MODEL), l),
            _resident(bco.shape),
            _layer((ATTN_WIDTH, D_MODEL), l),
            _layer((D_MODEL, D_MODEL), l),
            _resident(g.shape),
            _layer((D_MODEL, 2 * D_FF), 0),
            _layer((D_FF, D_MODEL), l),
            _resident((1, D_MODEL)),
        ],
        out_specs=pl.BlockSpec((tm, D_MODEL), row),
        out_shape=jax.ShapeDtypeStruct((tokens, D_MODEL), _F32),
        scratch_shapes=[pltpu.VMEM((tm, D_FF), _BF16)],
        compiler_params=_params(1),
        name="merge_ffn_final" if final else "merge_ffn",
    )(x, mod, y, o, gc, ga, lg, lb, wco, bco, wao, wout, g, w1, w2, fg)


def _rope_tables(seq):
    rows = seq // GRID_W
    row_pos = jnp.broadcast_to(jnp.arange(rows)[:, None], (rows, GRID_W)).reshape(-1).astype(_F32)
    col_pos = jnp.broadcast_to(jnp.arange(GRID_W)[None, :], (rows, GRID_W)).reshape(-1).astype(_F32)
    inv_freq = ROPE_THETA ** (-jnp.arange(0, ROPE_AXIS_DIM, 2, dtype=_F32) / ROPE_AXIS_DIM)
    ang_r = row_pos[:, None] * inv_freq[None, :]
    ang_c = col_pos[:, None] * inv_freq[None, :]
    cos_r, sin_r, cos_c, sin_c = jnp.cos(ang_r), jnp.sin(ang_r), jnp.cos(ang_c), jnp.sin(ang_c)
    zero = jnp.zeros_like(sin_r)
    cos = jnp.concatenate([cos_r, cos_r, cos_c, cos_c], axis=-1)
    sin_a = jnp.concatenate([-sin_r, zero, -sin_c, zero], axis=-1)
    sin_b = jnp.concatenate([zero, sin_r, zero, sin_c], axis=-1)
    reps = LANES // HEAD_DIM
    return jnp.tile(cos, (1, reps)), jnp.tile(sin_a, (1, reps)), jnp.tile(sin_b, (1, reps))


def _head_block_diag():
    head = jnp.arange(GROUP_WIDTH) // HEAD_DIM
    return (head[:, None] == head[None, :]).astype(_BF16)


def kernel(x, c, w_ada, b_ada, norm_mix_g, w_in, q_norm_g, k_norm_g, w_attn_o, conv_dw, conv_dw_b,
           conv_ln_g, conv_ln_b, w_conv_o, b_conv_o, w_out, norm_ffn_g, w_ffn_in, w_ffn_out,
           final_norm_g):
    batch, seq, d = x.shape
    depth = w_in.shape[0]
    assert d == D_MODEL and seq % TM_MIX == 0 and seq % TM_FFN == 0
    assert seq % TQ_STEP == 0 and TQ_STEP % TQ_SUB == 0 and seq % GRID_W == 0

    cos, sin_a, sin_b = _rope_tables(seq)
    bd = _head_block_diag()
    mod = _ada_call(c, w_ada, b_ada)
    reps = LANES // HEAD_DIM
    gq, gk = jnp.tile(q_norm_g, (1, reps)), jnp.tile(k_norm_g, (1, reps))
    g_mix, g_ffn, g_final = norm_mix_g, norm_ffn_g, final_norm_g.reshape(1, -1)
    cb, lg, lb, bco = conv_dw_b, conv_ln_g, conv_ln_b, b_conv_o

    xt = x.reshape(batch * seq, d)
    for l in range(depth):
        q, kexp, vaug, u, gc, ga, w_ffn_in_b = _mix_in_call(l, xt, mod, g_mix, w_in, bd, gq, gk,
                                                            cos, sin_a, sin_b, w_ffn_in, seq)
        o, y = _attn_call(l, q, kexp, vaug, u, conv_dw, cb, batch, seq)
        xt = _merge_ffn_call(l, xt, mod, y, o, gc, ga, lg, lb, w_conv_o, bco, w_attn_o, w_out,
                             g_ffn, w_ffn_in_b, w_ffn_out, g_final, seq, final=(l == depth - 1))
    return xt.reshape(batch, seq, d)
```
